```python
import jax, jax.numpy as jnp
from jax import lax
import numpy as np

D_MODEL = 1024
BATCH = 4
SEQ = 4096
DEPTH = 2
DEC_BATCH = 8
DEC_SEQ = 32
PAST_LEN = 2048

CHUNK = 64
HEAD_DIM = 64
A_HEADS = D_MODEL // HEAD_DIM
A_KV_HEADS = A_HEADS // 4
A_GROUP = A_HEADS // A_KV_HEADS
A_WINDOW = 128
A_PREV = A_WINDOW // CHUNK
B_HEADS = D_MODEL // HEAD_DIM
B_PREV = 8
B_REACH = B_PREV * CHUNK
REL_CLIP = 128
N_REL = 2 * REL_CLIP + 1
D_FF = 2816
CONV_W = 3
EPS = 1e-6
NEG_INF = -1e30
A_Q = A_HEADS * HEAD_DIM
A_KV = A_KV_HEADS * HEAD_DIM
B_W = B_HEADS * HEAD_DIM
N_IN = A_Q + 2 * A_KV + 3 * B_W + 2 * D_MODEL
SPLITS = (A_Q, A_Q + A_KV, A_Q + 2 * A_KV, A_Q + 2 * A_KV + B_W,
          A_Q + 2 * A_KV + 2 * B_W, A_Q + 2 * A_KV + 3 * B_W,
          A_Q + 2 * A_KV + 3 * B_W + D_MODEL)

kernel_name = 'hybrid_swa_sink_chunkband_convffn_step'


def rmsnorm(x, g):
    xf = x.astype(jnp.float32)
    y = xf * lax.rsqrt(jnp.mean(xf * xf, axis=-1, keepdims=True) + EPS)
    return (y * g.astype(jnp.float32)).astype(x.dtype)


def rel_dist(n_q, n_k, n_before):
    return jnp.arange(n_q)[:, None] + n_before - jnp.arange(n_k)[None, :]


def alibi_bias(dist):
    slopes = 2.0 ** (-8.0 * jnp.arange(1, A_HEADS + 1, dtype=jnp.float32) / A_HEADS)
    b = -slopes[:, None, None] * jnp.abs(dist).astype(jnp.float32)[None]
    return b.reshape(A_KV_HEADS, A_GROUP, dist.shape[0], dist.shape[1])


def rel_bias(table, dist):
    idx = jnp.clip(dist, -REL_CLIP, REL_CLIP) + REL_CLIP
    return jnp.transpose(table[idx].astype(jnp.float32), (2, 0, 1))[:, None]


def attend(q, k, v, bias, sink):
    s = jnp.einsum('bqngd,bknd->bngqk', q, k, preferred_element_type=jnp.float32) * (HEAD_DIM ** -0.5) + bias
    if sink is not None:
        col = jnp.broadcast_to(sink.astype(jnp.float32)[:, :, None, None], s.shape[:-1] + (1,))
        p = jax.nn.softmax(jnp.concatenate([s, col], axis=-1), axis=-1)[..., :-1]
    else:
        p = jax.nn.softmax(s, axis=-1)
    return jnp.einsum('bngqk,bknd->bqngd', p.astype(v.dtype), v)


def band_prompt(q, k, v, n_prev, bias, sink):
    b, s = q.shape[:2]
    n_chunks = s // CHUNK
    reach = n_prev * CHUNK
    span = reach + CHUNK
    pad = ((0, 0), (reach, 0), (0, 0), (0, 0))
    kp, vp = jnp.pad(k, pad), jnp.pad(v, pad)
    qc = jnp.swapaxes(q.reshape((b, n_chunks, CHUNK) + q.shape[2:]), 0, 1)

    def one_chunk(args):
        c, qi = args
        start = c * CHUNK
        kb = lax.dynamic_slice_in_dim(kp, start, span, axis=1)
        vb = lax.dynamic_slice_in_dim(vp, start, span, axis=1)
        valid = start - reach + jnp.arange(span) >= 0
        return attend(qi, kb, vb, jnp.where(valid, bias, NEG_INF), sink)

    out = lax.map(one_chunk, (jnp.arange(n_chunks), qc))
    return jnp.swapaxes(out, 0, 1).reshape(b, s, -1)


def project(h, w_in, b_gate, qa_g, ka_g, qb_g, kb_g):
    b, t = h.shape[:2]
    z = h @ w_in
    qa, ka, va, qb, kb, vb, ga, gb = jnp.split(z, SPLITS, axis=-1)
    heads = lambda u, n: u.reshape(b, t, n, HEAD_DIM)
    qa = rmsnorm(heads(qa, A_HEADS), qa_g).reshape(b, t, A_KV_HEADS, A_GROUP, HEAD_DIM)
    ka = rmsnorm(heads(ka, A_KV_HEADS), ka_g)
    va = heads(va, A_KV_HEADS)
    qb = rmsnorm(heads(qb, B_HEADS), qb_g)[:, :, :, None]
    kb = rmsnorm(heads(kb, B_HEADS), kb_g)
    vb = heads(vb, B_HEADS)
    ga = jax.nn.sigmoid(ga + b_gate[:D_MODEL])
    gb = jax.nn.sigmoid(gb + b_gate[D_MODEL:])
    return qa, ka, va, qb, kb, vb, ga, gb


def conv_ffn(h, w_up, conv_w, conv_b, w_down, prev):
    u = h @ w_up
    t = u.shape[1]
    up = jnp.concatenate([prev.astype(u.dtype), u], axis=1)
    c = conv_b + conv_w[0] * up[:, 0:t]
    for j in range(1, CONV_W):
        c = c + conv_w[j] * up[:, j:j + t]
    a, g = jnp.split(c, 2, axis=-1)
    return (jax.nn.gelu(a, approximate=False) * g) @ w_down, up[:, -(CONV_W - 1):]


def trunk_layer(x, p, cache):
    (n1, w_in, b_gate, qa_g, ka_g, qb_g, kb_g, sinks, table,
     w_out, n2, w_up, conv_w, conv_b, w_down) = p
    b, t, _ = x.shape
    h = rmsnorm(x, n1)
    qa, ka, va, qb, kb, vb, ga, gb = project(h, w_in, b_gate, qa_g, ka_g, qb_g, kb_g)
    sink = sinks.reshape(A_KV_HEADS, A_GROUP)
    if cache is None:
        span_a, span_b = (A_PREV + 1) * CHUNK, (B_PREV + 1) * CHUNK
        oa = band_prompt(qa, ka, va, A_PREV, alibi_bias(rel_dist(CHUNK, span_a, A_PREV * CHUNK)), sink)
        ob = band_prompt(qb, kb, vb, B_PREV, rel_bias(table, rel_dist(CHUNK, span_b, B_PREV * CHUNK)), None)
        conv_prev = jnp.zeros((b, CONV_W - 1, 2 * D_FF), x.dtype)
        wa, wb = min(A_WINDOW, t), min(B_REACH, t)
        new_kv = (ka[:, -wa:], va[:, -wa:], kb[:, -wb:], vb[:, -wb:])
    else:
        cak, cav, cbk, cbv, conv_prev = cache
        wa, wb = cak.shape[1], cbk.shape[1]
        oa = attend(qa, jnp.concatenate([cak.astype(ka.dtype), ka], axis=1),
                    jnp.concatenate([cav.astype(va.dtype), va], axis=1),
                    alibi_bias(rel_dist(t, wa + t, wa)), sink)
        ob = attend(qb, jnp.concatenate([cbk.astype(kb.dtype), kb], axis=1),
                    jnp.concatenate([cbv.astype(vb.dtype), vb], axis=1),
                    rel_bias(table, rel_dist(t, wb + t, wb)), None)
        new_kv = (ka, va, kb, vb)
    mixed = ga * oa.reshape(b, t, D_MODEL) + gb * ob.reshape(b, t, D_MODEL)
    x = x + mixed @ w_out
    f, conv_state = conv_ffn(rmsnorm(x, n2), w_up, conv_w, conv_b, w_down, conv_prev)
    x = x + f
    return x, (new_kv[0], new_kv[1], new_kv[2], new_kv[3], conv_state)


def setup_inputs(seed: int = 0) -> dict:
    key = jax.random.key(seed)
    ks = jax.random.split(key, 22)
    nrm = lambda k, shape, scale: scale * jax.random.normal(k, shape, jnp.float32)
    wa, wb = min(A_WINDOW, PAST_LEN), min(B_REACH, PAST_LEN)
    return {
        'x_prompt': nrm(ks[0], (BATCH, SEQ, D_MODEL), 1.0),
        'x_sample': nrm(ks[1], (DEC_BATCH, DEC_SEQ, D_MODEL), 1.0),
        'cache_a_k': nrm(ks[2], (DEPTH, DEC_BATCH, wa, A_KV_HEADS, HEAD_DIM), 1.0),
        'cache_a_v': nrm(ks[3], (DEPTH, DEC_BATCH, wa, A_KV_HEADS, HEAD_DIM), 1.0),
        'cache_b_k': nrm(ks[4], (DEPTH, DEC_BATCH, wb, B_HEADS, HEAD_DIM), 1.0),
        'cache_b_v': nrm(ks[5], (DEPTH, DEC_BATCH, wb, B_HEADS, HEAD_DIM), 1.0),
        'cache_ffn_conv': nrm(ks[6], (DEPTH, DEC_BATCH, CONV_W - 1, 2 * D_FF), 1.0),
        'norm1_g': 1.0 + nrm(ks[7], (DEPTH, D_MODEL), 0.02),
        'w_in': nrm(ks[8], (DEPTH, D_MODEL, N_IN), D_MODEL ** -0.5),
        'b_gate': nrm(ks[9], (DEPTH, 2 * D_MODEL), 0.02),
        'qn_a_g': 1.0 + nrm(ks[10], (DEPTH, HEAD_DIM), 0.02),
        'kn_a_g': 1.0 + nrm(ks[11], (DEPTH, HEAD_DIM), 0.02),
        'qn_b_g': 1.0 + nrm(ks[12], (DEPTH, HEAD_DIM), 0.02),
        'kn_b_g': 1.0 + nrm(ks[13], (DEPTH, HEAD_DIM), 0.02),
        'sinks_a': nrm(ks[14], (DEPTH, A_HEADS), 0.5),
        'rel_bias_b': nrm(ks[15], (DEPTH, N_REL, B_HEADS), 0.1),
        'w_out': nrm(ks[16], (DEPTH, D_MODEL, D_MODEL), D_MODEL ** -0.5),
        'norm2_g': 1.0 + nrm(ks[17], (DEPTH, D_MODEL), 0.02),
        'w_up': nrm(ks[18], (DEPTH, D_MODEL, 2 * D_FF), D_MODEL ** -0.5),
        'conv_w': nrm(ks[19], (DEPTH, CONV_W, 2 * D_FF), CONV_W ** -0.5),
        'conv_b': nrm(ks[20], (DEPTH, 2 * D_FF), 0.02),
        'w_down': nrm(ks[21], (DEPTH, D_FF, D_MODEL), D_FF ** -0.5),
    }


def reference(x_prompt, x_sample, cache_a_k, cache_a_v, cache_b_k, cache_b_v, cache_ffn_conv,
              norm1_g, w_in, b_gate, qn_a_g, kn_a_g, qn_b_g, kn_b_g, sinks_a, rel_bias_b,
              w_out, norm2_g, w_up, conv_w, conv_b, w_down):
    xp, xs = x_prompt, x_sample
    sp, ss = [], []
    for l in range(DEPTH):
        p = (norm1_g[l], w_in[l], b_gate[l], qn_a_g[l], kn_a_g[l], qn_b_g[l], kn_b_g[l],
             sinks_a[l], rel_bias_b[l], w_out[l], norm2_g[l], w_up[l], conv_w[l], conv_b[l], w_down[l])
        xp, st_p = trunk_layer(xp, p, None)
        sp.append(st_p)
        xs, st_s = trunk_layer(xs, p, (cache_a_k[l], cache_a_v[l], cache_b_k[l], cache_b_v[l],
                                       cache_ffn_conv[l]))
        ss.append(st_s)
    stk = lambda states, i: jnp.stack([s[i] for s in states])
    return (xp, xs,
            stk(sp, 0), stk(sp, 1), stk(sp, 2), stk(sp, 3), stk(sp, 4),
            stk(ss, 0), stk(ss, 1), stk(ss, 2), stk(ss, 3), stk(ss, 4))
```

```python
import functools

import jax
import jax.numpy as jnp
from jax import lax
from jax.experimental import pallas as pl
from jax.experimental.pallas import tpu as pltpu

HEAD_DIM = 64
CHUNK = 64
A_GROUP = 4
A_PREV = 2
B_PREV = 8
A_REACH = A_PREV * CHUNK
B_REACH = B_PREV * CHUNK
REL_CLIP = 128
CONV_W = 3
EPS = 1e-6
NEG_INF = -1e30
Q_SCALE = HEAD_DIM ** -0.5

LANES = 128
SUBLANES = 8
VMEM_LIMIT_BYTES = 56 * 1024 * 1024

PROJ_TILE = 512
ATTN_TILE = 256
FF_CHUNK = 256
REL_ROW = 1024

_BF16 = jnp.bfloat16
_F32 = jnp.float32


def _resident(shape, index_map):
    return pl.BlockSpec(shape, index_map, pipeline_mode=pl.Buffered(1))


def _params(n_axes):
    return pltpu.CompilerParams(dimension_semantics=("arbitrary",) * n_axes,
                                vmem_limit_bytes=VMEM_LIMIT_BYTES)


def _rmsnorm_rows(x, g):
    return x * lax.rsqrt(jnp.mean(x * x, axis=-1, keepdims=True) + EPS) * g


def _lane_is_low():
    return lax.broadcasted_iota(jnp.int32, (1, LANES), 1) < HEAD_DIM


def _headnorm_slab(z, g2):
    low = _lane_is_low()
    sq = z * z
    s_lo = jnp.sum(jnp.where(low, sq, 0.0), axis=-1, keepdims=True)
    s_hi = jnp.sum(jnp.where(low, 0.0, sq), axis=-1, keepdims=True)
    ms = jnp.where(low, s_lo, s_hi) * (1.0 / HEAD_DIM)
    return z * lax.rsqrt(ms + EPS) * g2


def _proj_kernel(x_ref, n1_ref, w_ref, bg_ref, gqa_ref, gka_ref, gqb_ref, gkb_ref,
                 qa_ref, qb_ref, g_ref, ka_ref, va_ref, kb_ref, vb_ref,
                 kat_ref, vat_ref, kbt_ref, vbt_ref, *, n_pad, n_steps, d_model, a_kv):
    j = pl.program_id(1)
    o_ka = d_model
    o_va = o_ka + a_kv
    o_qb = o_va + a_kv
    o_kb = o_qb + d_model
    o_vb = o_kb + d_model
    o_g = o_vb + d_model

    if n_pad:
        @pl.when(j < n_pad)
        def _():
            for r in (ka_ref, va_ref, kb_ref, vb_ref):
                r[...] = jnp.zeros(r.shape, r.dtype)

    @pl.when(j >= n_pad)
    def _():
        h = _rmsnorm_rows(x_ref[0], n1_ref[...]).astype(_BF16)
        is_tail = j == n_steps - 1

        def slab(col):
            return jnp.dot(h, w_ref[:, col:col + LANES], preferred_element_type=_F32)

        def emit(out_ref, tail_ref, base, width, gain_ref):
            for s in range(width // LANES):
                y = slab(base + s * LANES)
                if gain_ref is not None:
                    y = _headnorm_slab(y, gain_ref[...])
                out_ref[0, s] = y.astype(out_ref.dtype)
                if tail_ref is not None:
                    @pl.when(is_tail)
                    def _():
                        tail_ref[0, s] = y

        emit(qa_ref, None, 0, d_model, gqa_ref)
        emit(ka_ref, kat_ref, o_ka, a_kv, gka_ref)
        emit(va_ref, vat_ref, o_va, a_kv, None)
        emit(qb_ref, None, o_qb, d_model, gqb_ref)
        emit(kb_ref, kbt_ref, o_kb, d_model, gkb_ref)
        emit(vb_ref, vbt_ref, o_vb, d_model, None)
        for s in range(2 * d_model // LANES):
            c = s * LANES
            y = jax.nn.sigmoid(slab(o_g + c) + bg_ref[:, c:c + LANES])
            g_ref[0, s] = y.astype(g_ref.dtype)


def _proj(x, n1, w_in, b_gate, gqa, gka, gqb, gkb, *, tile, n_pad):
    bsz, seq, d_model = x.shape
    a_kv = d_model // A_GROUP
    n_tiles = seq // tile
    n_steps = n_tiles + n_pad
    pad_rows = n_pad * tile
    ns_d, ns_kv = d_model // LANES, a_kv // LANES

    def row(b, j):
        return (b, jnp.maximum(j - n_pad, 0), 0)

    def slab_row(b, j):
        return (b, 0, jnp.maximum(j - n_pad, 0), 0)

    def slab_padded(b, j):
        return (b, 0, j, 0)

    def const2(b, j):
        return (0, 0)

    def tail(b, j):
        return (b, 0, 0, 0)

    bf = lambda ns, rows: jax.ShapeDtypeStruct((bsz, ns, rows, LANES), _BF16)
    f32 = lambda ns: jax.ShapeDtypeStruct((bsz, ns, tile, LANES), _F32)
    out_shape = (bf(ns_d, seq), bf(ns_d, seq), bf(2 * ns_d, seq),
                 bf(ns_kv, pad_rows + seq), bf(ns_kv, pad_rows + seq),
                 bf(ns_d, pad_rows + seq), bf(ns_d, pad_rows + seq),
                 f32(ns_kv), f32(ns_kv), f32(ns_d), f32(ns_d))
    blk = lambda ns, imap: pl.BlockSpec((1, ns, tile, LANES), imap)
    out_specs = (blk(ns_d, slab_row), blk(ns_d, slab_row), blk(2 * ns_d, slab_row),
                 blk(ns_kv, slab_padded), blk(ns_kv, slab_padded), blk(ns_d, slab_padded), blk(ns_d, slab_padded),
                 blk(ns_kv, tail), blk(ns_kv, tail), blk(ns_d, tail), blk(ns_d, tail))
    in_specs = [pl.BlockSpec((1, tile, d_model), row),
                _resident((1, d_model), const2),
                _resident(w_in.shape, const2),
                _resident((1, 2 * d_model), const2),
                _resident((1, LANES), const2), _resident((1, LANES), const2),
                _resident((1, LANES), const2), _resident((1, LANES), const2)]
    kern = functools.partial(_proj_kernel, n_pad=n_pad, n_steps=n_steps, d_model=d_model, a_kv=a_kv)
    return pl.pallas_call(kern, grid=(bsz, n_steps), in_specs=in_specs, out_specs=out_specs,
                          out_shape=out_shape, compiler_params=_params(2), name="proj")(
                              x, n1, w_in, b_gate, gqa, gka, gqb, gkb)


def _band_mask(tq, w, n_prev):
    shift = CHUNK.bit_length() - 1
    qc = lax.shift_right_logical(lax.broadcasted_iota(jnp.int32, (tq, w), 0), shift)
    kc = lax.shift_right_logical(lax.broadcasted_iota(jnp.int32, (tq, w), 1), shift)
    d = kc - qc
    return (d >= 0) & (d <= n_prev)


def _window(reach, tq):
    return -(-(reach + tq) // LANES) * LANES


def _swap_halves(x):
    return jnp.concatenate([x[:, HEAD_DIM:], x[:, :HEAD_DIM]], axis=-1)


def _attn_kernel(slopes_ref, sinks_ref, rel_ref, qa_ref, qb_ref, g_ref,
                 ka_ref, va_ref, kb_ref, vb_ref, o_ref, bias_a, bias_b, mix,
                 *, tq, wa, wb, windowed, n_slabs):
    b = pl.program_id(0)
    i = pl.program_id(1)

    @pl.when((b == 0) & (i == 0))
    def _():
        qpos = lax.broadcasted_iota(jnp.int32, (tq, wa), 0)
        kpos = lax.broadcasted_iota(jnp.int32, (tq, wa), 1) - A_REACH
        dist = jnp.abs(qpos - kpos).astype(_F32)
        band_a = _band_mask(tq, wa, A_PREV)
        band_b = _band_mask(tq, wb, B_PREV)
        for h in range(2 * n_slabs):
            bias_a[h] = jnp.where(band_a, -slopes_ref[h] * dist, NEG_INF)
            rows = jnp.broadcast_to(rel_ref[h:h + 1, :], (tq, REL_ROW))
            rows = pltpu.roll(rows, REL_ROW - ATTN_TILE + 1, 1, stride=1, stride_axis=0)
            bias_b[h] = jnp.where(band_b, rows[:, :wb], NEG_INF)

    col_a = lax.broadcasted_iota(jnp.int32, (1, wa), 1)
    col_b = lax.broadcasted_iota(jnp.int32, (1, wb), 1)
    if windowed:
        start = pl.multiple_of(i * tq, tq)
        sa = start + (B_REACH - A_REACH)
        sb = start
        valid_a = jnp.where(col_a >= A_REACH - i * tq, 0.0, NEG_INF)
        valid_b = jnp.where(col_b >= B_REACH - i * tq, 0.0, NEG_INF)
    else:
        sa = sb = 0
        valid_a = jnp.where(col_a < A_REACH + tq, 0.0, NEG_INF)
        valid_b = jnp.where(col_b < B_REACH + tq, 0.0, NEG_INF)

    low = _lane_is_low()

    def attend(qm, k, v, bias, valid, sink):
        s = lax.dot_general(qm, k, (((1,), (1,)), ((), ())), preferred_element_type=_F32) + bias + valid
        m = jnp.max(s, axis=-1, keepdims=True)
        if sink is not None:
            m = jnp.maximum(m, sink)
        p = jnp.exp(s - m)
        l = jnp.sum(p, axis=-1, keepdims=True)
        if sink is not None:
            l = l + jnp.exp(sink - m)
        return jnp.dot(p.astype(_BF16), v, preferred_element_type=_F32) / l

    def head_pair(qs, k_lo, v_lo, k_hi, v_hi, bias, first_head, valid, sinks):
        zero = jnp.zeros_like(qs)
        o_lo = attend(jnp.where(low, qs, zero), k_lo, v_lo, bias[first_head], valid,
                      None if sinks is None else sinks[first_head])
        o_hi = attend(jnp.where(low, zero, qs), k_hi, v_hi, bias[first_head + 1], valid,
                      None if sinks is None else sinks[first_head + 1])
        return jnp.where(low, o_lo, o_hi)

    for kvh in range(n_slabs // 2):
        k = ka_ref[0, kvh // 2, pl.ds(sa, wa), :]
        v = va_ref[0, kvh // 2, pl.ds(sa, wa), :]
        k_sw, v_sw = _swap_halves(k), _swap_halves(v)
        k_lo, v_lo, k_hi, v_hi = (k, v, k_sw, v_sw) if kvh % 2 == 0 else (k_sw, v_sw, k, v)

        def body_a(t, carry, kvh=kvh, k_lo=k_lo, v_lo=v_lo, k_hi=k_hi, v_hi=v_hi):
            slab = 2 * kvh + t
            oa = head_pair(qa_ref[0, slab], k_lo, v_lo, k_hi, v_hi, bias_a, 2 * slab, valid_a, sinks_ref)
            mix[slab] = g_ref[0, slab].astype(_F32) * oa
            return carry

        lax.fori_loop(0, 2, body_a, 0)

    def body_b(slab, carry):
        k = kb_ref[0, slab, pl.ds(sb, wb), :]
        v = vb_ref[0, slab, pl.ds(sb, wb), :]
        ob = head_pair(qb_ref[0, slab], k, v, k, v, bias_b, 2 * slab, valid_b, None)
        o_ref[0, slab] = (mix[slab] + g_ref[0, n_slabs + slab].astype(_F32) * ob).astype(o_ref.dtype)
        return carry

    lax.fori_loop(0, n_slabs, body_b, 0)


def _attention(slopes, sinks, rel_rows, qa, qb, gates, ka, va, kb, vb, *, tq, windowed):
    bsz, n_slabs, seq, _ = qa.shape
    wa = _window(A_REACH, tq)
    wb = _window(B_REACH, tq)
    n_tiles = seq // tq

    def row(b, i):
        return (b, 0, i, 0)

    def per_batch(b, i):
        return (b, 0, 0, 0)

    def const2(b, i):
        return (0, 0)

    smem = pl.BlockSpec(memory_space=pltpu.SMEM)
    rows = lambda ns: pl.BlockSpec((1, ns, tq, LANES), row)
    whole = lambda t: _resident((1,) + t.shape[1:], per_batch)
    in_specs = [smem, smem, _resident(rel_rows.shape, const2),
                rows(n_slabs), rows(n_slabs), rows(2 * n_slabs),
                whole(ka), whole(va), whole(kb), whole(vb)]
    kern = functools.partial(_attn_kernel, tq=tq, wa=wa, wb=wb, windowed=windowed, n_slabs=n_slabs)
    return pl.pallas_call(
        kern, grid=(bsz, n_tiles), in_specs=in_specs, out_specs=rows(n_slabs),
        out_shape=jax.ShapeDtypeStruct(qa.shape, _BF16),
        scratch_shapes=[pltpu.VMEM((2 * n_slabs, tq, wa), _F32), pltpu.VMEM((2 * n_slabs, tq, wb), _F32),
                        pltpu.VMEM((n_slabs, tq, LANES), _F32)],
        compiler_params=_params(2), name="attn")(slopes, sinks, rel_rows, qa, qb, gates, ka, va, kb, vb)


def _gelu(x):
    return 0.5 * x * (1.0 + lax.erf(x * (2.0 ** -0.5)))


def _ffn_kernel(x_ref, m_ref, wo_ref, n2_ref, wu_ref, cw_ref, cb_ref, wd_ref, prev_ref,
                y_ref, st_ref, carry, act, *, d_ff, carried):
    i = pl.program_id(1)
    rows = x_ref.shape[1]

    if carried:
        @pl.when(i == 0)
        def _():
            carry[...] = jnp.zeros(carry.shape, carry.dtype)

    mixed = jnp.concatenate([m_ref[0, s] for s in range(m_ref.shape[1])], axis=-1)
    x1 = x_ref[0] + jnp.dot(mixed, wo_ref[...], preferred_element_type=_F32)
    h = _rmsnorm_rows(x1, n2_ref[...]).astype(_BF16)
    r = lax.broadcasted_iota(jnp.int32, (rows, 1), 0)

    def conv(col):
        u = jnp.dot(h, wu_ref[:, col:col + FF_CHUNK], preferred_element_type=_F32)
        cs = slice(col, col + FF_CHUNK)
        if carried:
            p0, p1 = carry[SUBLANES - 2:SUBLANES - 1, cs], carry[SUBLANES - 1:SUBLANES, cs]
            carry[:, cs] = u[rows - SUBLANES:, :]
        else:
            p0, p1 = prev_ref[0, 0:1, cs], prev_ref[0, 1:2, cs]
        st_ref[0, :, cs] = u[rows - SUBLANES:, :]
        u1 = jnp.where(r == 0, p1, pltpu.roll(u, 1, 0))
        u2 = jnp.where(r == 0, p0, jnp.where(r == 1, p1, pltpu.roll(u, 2, 0)))
        return cb_ref[:, cs] + cw_ref[0:1, cs] * u2 + cw_ref[1:2, cs] * u1 + cw_ref[2:3, cs] * u

    for c in range(0, d_ff, FF_CHUNK):
        a = conv(c)
        g = conv(d_ff + c)
        act[:, c:c + FF_CHUNK] = (_gelu(a) * g).astype(act.dtype)

    y_ref[0] = x1 + jnp.dot(act[...], wd_ref[...], preferred_element_type=_F32)


def _ffn(x, mixed, w_out, n2, w_up, conv_w, conv_b, w_down, prev, *, tile, carried):
    bsz, seq, d_model = x.shape
    d_ff = w_down.shape[0]
    n_tiles = seq // tile

    def row(b, i):
        return (b, i, 0)

    def slab_row(b, i):
        return (b, 0, i, 0)

    def per_batch(b, i):
        return (b, 0, 0)

    def const2(b, i):
        return (0, 0)

    in_specs = [pl.BlockSpec((1, tile, d_model), row),
                pl.BlockSpec((1, mixed.shape[1], tile, LANES), slab_row),
                _resident(w_out.shape, const2), _resident((1, d_model), const2),
                _resident(w_up.shape, const2), _resident(conv_w.shape, const2),
                _resident((1, 2 * d_ff), const2), _resident(w_down.shape, const2),
                pl.BlockSpec((1,) + prev.shape[1:], per_batch)]
    out_specs = (pl.BlockSpec((1, tile, d_model), row), pl.BlockSpec((1, SUBLANES, 2 * d_ff), per_batch))
    out_shape = (jax.ShapeDtypeStruct((bsz, seq, d_model), _F32),
                 jax.ShapeDtypeStruct((bsz, SUBLANES, 2 * d_ff), _F32))
    kern = functools.partial(_ffn_kernel, d_ff=d_ff, carried=carried)
    return pl.pallas_call(
        kern, grid=(bsz, n_tiles), in_specs=in_specs, out_specs=out_specs, out_shape=out_shape,
        scratch_shapes=[pltpu.VMEM((SUBLANES, 2 * d_ff), _F32), pltpu.VMEM((tile, d_ff), _BF16)],
        compiler_params=_params(2), name="ffn")(x, mixed, w_out, n2, w_up, conv_w, conv_b, w_down, prev)


def _rel_rows(table):
    n_low = B_REACH - REL_CLIP + ATTN_TILE - 1
    n_high = REL_ROW - n_low - table.shape[0]
    rows = jnp.concatenate([jnp.broadcast_to(table[-1:], (n_low, table.shape[1])),
                            table[::-1],
                            jnp.broadcast_to(table[:1], (n_high, table.shape[1]))], axis=0)
    return rows.T


def _tile2(g, scale=1.0):
    return (jnp.tile(g, 2) * scale).reshape(1, LANES)


def _rows_major(t):
    b, ns, r, _ = t.shape
    return jnp.swapaxes(t, 1, 2).reshape(b, r, ns * LANES)


def _slab_major(t):
    b, r, c = t.shape
    return jnp.swapaxes(t.reshape(b, r, c // LANES, LANES), 1, 2)


def _split_batches(t, nb):
    _, ns, rows, _ = t.shape
    return jnp.swapaxes(t.reshape(ns, nb, rows // nb, LANES), 0, 1)


def kernel(x_prompt, x_sample, cache_a_k, cache_a_v, cache_b_k, cache_b_v, cache_ffn_conv, norm1_g, w_in, b_gate, qn_a_g, kn_a_g, qn_b_g, kn_b_g, sinks_a, rel_bias_b, w_out, norm2_g, w_up, conv_w, conv_b, w_down):
    depth = w_in.shape[0]
    bsz, seq, d_model = x_prompt.shape
    dbs, dseq, _ = x_sample.shape
    n_heads = d_model // HEAD_DIM
    kv_heads = n_heads // A_GROUP
    two_ff = w_up.shape[-1]
    assert cache_a_k.shape[2] == A_REACH and cache_b_k.shape[2] == B_REACH
    assert seq % PROJ_TILE == 0 and PROJ_TILE == B_REACH and dseq <= CHUNK

    slopes = 2.0 ** (-8.0 * jnp.arange(1, n_heads + 1, dtype=_F32) / n_heads)
    xp, xs = x_prompt, x_sample.reshape(1, dbs * dseq, d_model)
    zero_prev = jnp.zeros((bsz, CONV_W - 1, two_ff), _F32)
    heads = lambda t, n: _rows_major(t).reshape(t.shape[0], t.shape[2], n, HEAD_DIM)
    pk, sk = [], []
    for l in range(depth):
        w_in_l, w_out_l = w_in[l].astype(_BF16), w_out[l].astype(_BF16)
        w_up_l, w_down_l = w_up[l].astype(_BF16), w_down[l].astype(_BF16)
        n1, n2 = norm1_g[l].reshape(1, -1), norm2_g[l].reshape(1, -1)
        bg, cb = b_gate[l].reshape(1, -1), conv_b[l].reshape(1, -1)
        gains = (_tile2(qn_a_g[l], Q_SCALE), _tile2(kn_a_g[l]), _tile2(qn_b_g[l], Q_SCALE), _tile2(kn_b_g[l]))
        rel = _rel_rows(rel_bias_b[l])

        qa, qb, g, ka, va, kb, vb, kat, vat, kbt, vbt = _proj(
            xp, n1, w_in_l, bg, *gains, tile=PROJ_TILE, n_pad=1)
        mixed = _attention(slopes, sinks_a[l], rel, qa, qb, g, ka, va, kb, vb, tq=ATTN_TILE, windowed=True)
        xp, st = _ffn(xp, mixed, w_out_l, n2, w_up_l, conv_w[l], cb, w_down_l, zero_prev,
                      tile=PROJ_TILE, carried=True)
        pk.append((heads(kat, kv_heads)[:, -A_REACH:], heads(vat, kv_heads)[:, -A_REACH:],
                   heads(kbt, n_heads), heads(vbt, n_heads), st[:, -(CONV_W - 1):]))

        qa, qb, g, ka, va, kb, vb, kat, vat, kbt, vbt = _proj(
            xs, n1, w_in_l, bg, *gains, tile=dbs * dseq, n_pad=0)

        def win(cache, new):
            rows = _window(cache.shape[1], dseq)
            old = _slab_major(cache.reshape(dbs, cache.shape[1], -1).astype(_BF16))
            w = jnp.concatenate([old, _split_batches(new, dbs)], axis=2)
            return jnp.pad(w, ((0, 0), (0, 0), (0, rows - w.shape[2]), (0, 0)))

        mixed = _attention(slopes, sinks_a[l], rel, _split_batches(qa, dbs), _split_batches(qb, dbs),
                           _split_batches(g, dbs),
                           win(cache_a_k[l], ka), win(cache_a_v[l], va),
                           win(cache_b_k[l], kb), win(cache_b_v[l], vb), tq=dseq, windowed=False)
        ys, st = _ffn(xs.reshape(dbs, dseq, d_model), mixed, w_out_l, n2, w_up_l, conv_w[l], cb, w_down_l,
                      cache_ffn_conv[l], tile=dseq, carried=False)
        xs = ys.reshape(1, dbs * dseq, d_model)
        per_b = lambda t, n: heads(t, n).reshape(dbs, dseq, n, HEAD_DIM)
        sk.append((per_b(kat, kv_heads), per_b(vat, kv_heads), per_b(kbt, n_heads), per_b(vbt, n_heads),
                   st[:, -(CONV_W - 1):]))

    stk = lambda states, i: jnp.stack([s[i] for s in states])
    return (xp, xs.reshape(dbs, dseq, d_model),
            stk(pk, 0), stk(pk, 1), stk(pk, 2), stk(pk, 3), stk(pk, 4),
            stk(sk, 0), stk(sk, 1), stk(sk, 2), stk(sk, 3), stk(sk, 4))
```

```python
import functools

import jax
import jax.numpy as jnp
from jax import lax
from jax.experimental import pallas as pl
from jax.experimental.pallas import tpu as pltpu

HEAD_DIM = 64
CHUNK = 64
A_GROUP = 4
A_PREV = 2
B_PREV = 8
A_REACH = A_PREV * CHUNK
B_REACH = B_PREV * CHUNK
REL_CLIP = 128
CONV_W = 3
EPS = 1e-6
NEG_INF = -1e30
Q_SCALE = HEAD_DIM ** -0.5

LANES = 128
SUBLANES = 8
MXU_WIDTH = 256
VMEM_LIMIT_BYTES = 56 * 1024 * 1024

PROJ_TILE = 512
ATTN_TILE = 256
ATTN_SUB = 128
FF_CHUNK = 256
REL_ROW = 1024

_BF16 = jnp.bfloat16
_F32 = jnp.float32


def _resident(shape, index_map):
    return pl.BlockSpec(shape, index_map, pipeline_mode=pl.Buffered(1))


def _params(n_axes):
    return pltpu.CompilerParams(dimension_semantics=("arbitrary",) * n_axes,
                                vmem_limit_bytes=VMEM_LIMIT_BYTES)


def _rmsnorm_rows(x, g):
    return x * lax.rsqrt(jnp.mean(x * x, axis=-1, keepdims=True) + EPS) * g


def _lane_is_low():
    return lax.broadcasted_iota(jnp.int32, (1, LANES), 1) < HEAD_DIM


def _headnorm_slab(z, g2):
    low = _lane_is_low()
    sq = z * z
    s_lo = jnp.sum(jnp.where(low, sq, 0.0), axis=-1, keepdims=True)
    s_hi = jnp.sum(jnp.where(low, 0.0, sq), axis=-1, keepdims=True)
    ms = jnp.where(low, s_lo, s_hi) * (1.0 / HEAD_DIM)
    return z * lax.rsqrt(ms + EPS) * g2


def _proj_kernel(x_ref, n1_ref, w_ref, bg_ref, gqa_ref, gka_ref, gqb_ref, gkb_ref,
                 qa_ref, qb_ref, g_ref, ka_ref, va_ref, kb_ref, vb_ref,
                 kat_ref, vat_ref, kbt_ref, vbt_ref, *, n_pad, d_model, a_kv):
    j = pl.program_id(1)
    o_ka = d_model
    o_va = o_ka + a_kv
    o_qb = o_va + a_kv
    o_kb = o_qb + d_model
    o_vb = o_kb + d_model
    o_g = o_vb + d_model

    if n_pad:
        @pl.when(j < n_pad)
        def _():
            for r in (ka_ref, va_ref, kb_ref, vb_ref):
                r[...] = jnp.zeros(r.shape, r.dtype)

    @pl.when(j >= n_pad)
    def _():
        h = _rmsnorm_rows(x_ref[0], n1_ref[...]).astype(_BF16)
        low = _lane_is_low()

        def slabs(base, width):
            for c in range(0, width, MXU_WIDTH):
                z = jnp.dot(h, w_ref[:, base + c:base + c + MXU_WIDTH], preferred_element_type=_F32)
                for t in range(MXU_WIDTH // LANES):
                    yield c // LANES + t, z[:, t * LANES:(t + 1) * LANES]

        def emit(out_ref, base, width, gain_ref=None, tail_ref=None, twice=False):
            for s, y in slabs(base, width):
                if gain_ref is not None:
                    y = _headnorm_slab(y, gain_ref[...])
                if tail_ref is not None:
                    tail_ref[0, s] = y
                if twice:
                    y_sw = pltpu.roll(y, HEAD_DIM, 1)
                    out_ref[0, 2 * s] = jnp.where(low, y, y_sw).astype(out_ref.dtype)
                    out_ref[0, 2 * s + 1] = jnp.where(low, y_sw, y).astype(out_ref.dtype)
                else:
                    out_ref[0, s] = y.astype(out_ref.dtype)

        emit(qa_ref, 0, d_model, gqa_ref)
        emit(ka_ref, o_ka, a_kv, gka_ref, kat_ref, twice=True)
        emit(va_ref, o_va, a_kv, None, vat_ref, twice=True)
        emit(qb_ref, o_qb, d_model, gqb_ref)
        emit(kb_ref, o_kb, d_model, gkb_ref, kbt_ref)
        emit(vb_ref, o_vb, d_model, None, vbt_ref)
        for s, y in slabs(o_g, 2 * d_model):
            c = s * LANES
            g_ref[0, s] = jax.nn.sigmoid(y + bg_ref[:, c:c + LANES]).astype(g_ref.dtype)


def _proj(x, n1, w_in, b_gate, gqa, gka, gqb, gkb, *, tile, n_pad):
    bsz, seq, d_model = x.shape
    a_kv = d_model // A_GROUP
    n_tiles = seq // tile
    n_steps = n_tiles + n_pad
    pad_rows = n_pad * tile
    ns_d, ns_kv = d_model // LANES, a_kv // LANES

    def row(b, j):
        return (b, jnp.maximum(j - n_pad, 0), 0)

    def slab_row(b, j):
        return (b, 0, jnp.maximum(j - n_pad, 0), 0)

    def slab_padded(b, j):
        return (b, 0, j, 0)

    def const2(b, j):
        return (0, 0)

    def tail(b, j):
        return (b, 0, 0, 0)

    bf = lambda ns, rows: jax.ShapeDtypeStruct((bsz, ns, rows, LANES), _BF16)
    f32 = lambda ns: jax.ShapeDtypeStruct((bsz, ns, tile, LANES), _F32)
    out_shape = (bf(ns_d, seq), bf(ns_d, seq), bf(2 * ns_d, seq),
                 bf(2 * ns_kv, pad_rows + seq), bf(2 * ns_kv, pad_rows + seq),
                 bf(ns_d, pad_rows + seq), bf(ns_d, pad_rows + seq),
                 f32(ns_kv), f32(ns_kv), f32(ns_d), f32(ns_d))
    blk = lambda ns, imap: pl.BlockSpec((1, ns, tile, LANES), imap)
    out_specs = (blk(ns_d, slab_row), blk(ns_d, slab_row), blk(2 * ns_d, slab_row),
                 blk(2 * ns_kv, slab_padded), blk(2 * ns_kv, slab_padded),
                 blk(ns_d, slab_padded), blk(ns_d, slab_padded),
                 blk(ns_kv, tail), blk(ns_kv, tail), blk(ns_d, tail), blk(ns_d, tail))
    in_specs = [pl.BlockSpec((1, tile, d_model), row),
                _resident((1, d_model), const2),
                _resident(w_in.shape, const2),
                _resident((1, 2 * d_model), const2),
                _resident((1, LANES), const2), _resident((1, LANES), const2),
                _resident((1, LANES), const2), _resident((1, LANES), const2)]
    kern = functools.partial(_proj_kernel, n_pad=n_pad, d_model=d_model, a_kv=a_kv)
    return pl.pallas_call(kern, grid=(bsz, n_steps), in_specs=in_specs, out_specs=out_specs,
                          out_shape=out_shape, compiler_params=_params(2), name="proj")(
                              x, n1, w_in, b_gate, gqa, gka, gqb, gkb)


def _band_mask(tq, w, n_prev):
    shift = CHUNK.bit_length() - 1
    qc = lax.shift_right_logical(lax.broadcasted_iota(jnp.int32, (tq, w), 0), shift)
    kc = lax.shift_right_logical(lax.broadcasted_iota(jnp.int32, (tq, w), 1), shift)
    d = kc - qc
    return (d >= 0) & (d <= n_prev)


def _window(reach, tq):
    return -(-(reach + tq) // LANES) * LANES


def _attn_kernel(slopes_ref, sinks_ref, rel_ref, qa_ref, qb_ref, g_ref,
                 ka_ref, va_ref, kb_ref, vb_ref, o_ref, bias_a, bias_b,
                 *, tq, ts, wa, wb, windowed, n_slabs):
    b = pl.program_id(0)
    i = pl.program_id(1)

    @pl.when((b == 0) & (i == 0))
    def _():
        qpos = lax.broadcasted_iota(jnp.int32, (ts, wa), 0)
        kpos = lax.broadcasted_iota(jnp.int32, (ts, wa), 1) - A_REACH
        dist = jnp.abs(qpos - kpos).astype(_F32)
        band_a = _band_mask(ts, wa, A_PREV)
        band_b = _band_mask(ts, wb, B_PREV)
        for h in range(2 * n_slabs):
            rows_h = slice((h % 2) * ts, (h % 2 + 1) * ts)
            bias_a[h // 2, rows_h, :] = jnp.where(band_a, -slopes_ref[h] * dist, NEG_INF)
            rows = jnp.broadcast_to(rel_ref[h:h + 1, :], (ts, REL_ROW))
            rows = pltpu.roll(rows, REL_ROW - ATTN_SUB + 1, 1, stride=1, stride_axis=0)
            bias_b[h // 2, rows_h, :] = jnp.where(band_b, rows[:, :wb], NEG_INF)

    low = _lane_is_low()
    row_is_lo = lax.broadcasted_iota(jnp.int32, (2 * ts, 1), 0) < ts
    col_a = lax.broadcasted_iota(jnp.int32, (1, wa), 1)
    col_b = lax.broadcasted_iota(jnp.int32, (1, wb), 1)
    ones = jnp.ones((max(wa, wb), LANES), _BF16)

    def slab_out(qs, k, v, bias, valid, sink_col):
        zero = jnp.zeros_like(qs)
        qq = jnp.concatenate([jnp.where(low, qs, zero), jnp.where(low, zero, qs)], axis=0)
        s = lax.dot_general(qq, k, (((1,), (1,)), ((), ())), preferred_element_type=_F32) + bias + valid
        m = jnp.max(s, axis=-1, keepdims=True)
        if sink_col is not None:
            m = jnp.maximum(m, sink_col)
        p = jnp.exp(s - m).astype(_BF16)
        o2 = jnp.dot(p, jnp.concatenate([v, ones[:v.shape[0]]], axis=1), preferred_element_type=_F32)
        den = o2[:, LANES:]
        if sink_col is not None:
            den = den + jnp.exp(sink_col - m)
        o = o2[:, :LANES] / den
        return jnp.where(low, o[:ts], o[ts:])

    subs = []
    for u in range(tq // ts):
        if windowed:
            start = pl.multiple_of(i * tq, tq) + u * ts
            subs.append((slice(u * ts, (u + 1) * ts), start + (B_REACH - A_REACH), start,
                         jnp.where(col_a >= A_REACH - start, 0.0, NEG_INF),
                         jnp.where(col_b >= B_REACH - start, 0.0, NEG_INF)))
        else:
            subs.append((slice(u * ts, (u + 1) * ts), 0, 0,
                         jnp.where(col_a < A_REACH + tq, 0.0, NEG_INF),
                         jnp.where(col_b < B_REACH + tq, 0.0, NEG_INF)))

    def body(s, carry):
        kv = lax.div(s, 2)
        sink_col = jnp.where(row_is_lo, sinks_ref[2 * s], sinks_ref[2 * s + 1])
        for rows_u, sa, sb, valid_a, valid_b in subs:
            oa = slab_out(qa_ref[0, s, rows_u], ka_ref[0, kv, pl.ds(sa, wa)], va_ref[0, kv, pl.ds(sa, wa)],
                          bias_a[s], valid_a, sink_col)
            ob = slab_out(qb_ref[0, s, rows_u], kb_ref[0, s, pl.ds(sb, wb)], vb_ref[0, s, pl.ds(sb, wb)],
                          bias_b[s], valid_b, None)
            mixed = g_ref[0, s, rows_u].astype(_F32) * oa + g_ref[0, n_slabs + s, rows_u].astype(_F32) * ob
            o_ref[0, s, rows_u] = mixed.astype(o_ref.dtype)
        return carry

    lax.fori_loop(0, n_slabs, body, 0, unroll=4)


def _attention(slopes, sinks, rel_rows, qa, qb, gates, ka, va, kb, vb, *, tq, windowed):
    bsz, n_slabs, seq, _ = qa.shape
    ts = min(tq, ATTN_SUB)
    wa = _window(A_REACH, ts)
    wb = _window(B_REACH, ts)
    n_tiles = seq // tq

    def row(b, i):
        return (b, 0, i, 0)

    def per_batch(b, i):
        return (b, 0, 0, 0)

    def const2(b, i):
        return (0, 0)

    smem = pl.BlockSpec(memory_space=pltpu.SMEM)
    rows = lambda ns: pl.BlockSpec((1, ns, tq, LANES), row)
    whole = lambda t: _resident((1,) + t.shape[1:], per_batch)
    in_specs = [smem, smem, _resident(rel_rows.shape, const2),
                rows(n_slabs), rows(n_slabs), rows(2 * n_slabs),
                whole(ka), whole(va), whole(kb), whole(vb)]
    kern = functools.partial(_attn_kernel, tq=tq, ts=ts, wa=wa, wb=wb, windowed=windowed, n_slabs=n_slabs)
    return pl.pallas_call(
        kern, grid=(bsz, n_tiles), in_specs=in_specs, out_specs=rows(n_slabs),
        out_shape=jax.ShapeDtypeStruct(qa.shape, _BF16),
        scratch_shapes=[pltpu.VMEM((n_slabs, 2 * ts, wa), _F32), pltpu.VMEM((n_slabs, 2 * ts, wb), _F32)],
        compiler_params=_params(2), name="attn")(slopes, sinks, rel_rows, qa, qb, gates, ka, va, kb, vb)


def _gelu(x):
    return 0.5 * x * (1.0 + lax.erf(x * (2.0 ** -0.5)))


def _ffn_kernel(x_ref, m_ref, wo_ref, n2_ref, wu_ref, cw_ref, cb_ref, wd_ref, prev_ref,
                y_ref, st_ref, carry, act, *, d_ff, carried):
    i = pl.program_id(1)
    rows = x_ref.shape[1]

    if carried:
        @pl.when(i == 0)
        def _():
            carry[...] = jnp.zeros(carry.shape, carry.dtype)

    mixed = jnp.concatenate([m_ref[0, s] for s in range(m_ref.shape[1])], axis=-1)
    x1 = x_ref[0] + jnp.dot(mixed, wo_ref[...], preferred_element_type=_F32)
    h = _rmsnorm_rows(x1, n2_ref[...]).astype(_BF16)
    r = lax.broadcasted_iota(jnp.int32, (rows, 1), 0)

    def conv(col):
        u = jnp.dot(h, wu_ref[:, col:col + FF_CHUNK], preferred_element_type=_F32)
        cs = slice(col, col + FF_CHUNK)
        if carried:
            p0, p1 = carry[SUBLANES - 2:SUBLANES - 1, cs], carry[SUBLANES - 1:SUBLANES, cs]
            carry[:, cs] = u[rows - SUBLANES:, :]
        else:
            p0, p1 = prev_ref[0, 0:1, cs], prev_ref[0, 1:2, cs]
        st_ref[0, :, cs] = u[rows - SUBLANES:, :]
        u1 = jnp.where(r == 0, p1, pltpu.roll(u, 1, 0))
        u2 = jnp.where(r == 0, p0, jnp.where(r == 1, p1, pltpu.roll(u, 2, 0)))
        return cb_ref[:, cs] + cw_ref[0:1, cs] * u2 + cw_ref[1:2, cs] * u1 + cw_ref[2:3, cs] * u

    for c in range(0, d_ff, FF_CHUNK):
        a = conv(c)
        g = conv(d_ff + c)
        act[:, c:c + FF_CHUNK] = (_gelu(a) * g).astype(act.dtype)

    y_ref[0] = x1 + jnp.dot(act[...], wd_ref[...], preferred_element_type=_F32)


def _ffn(x, mixed, w_out, n2, w_up, conv_w, conv_b, w_down, prev, *, tile, carried):
    bsz, seq, d_model = x.shape
    d_ff = w_down.shape[0]
    n_tiles = seq // tile

    def row(b, i):
        return (b, i, 0)

    def slab_row(b, i):
        return (b, 0, i, 0)

    def per_batch(b, i):
        return (b, 0, 0)

    def const2(b, i):
        return (0, 0)

    in_specs = [pl.BlockSpec((1, tile, d_model), row),
                pl.BlockSpec((1, mixed.shape[1], tile, LANES), slab_row),
                _resident(w_out.shape, const2), _resident((1, d_model), const2),
                _resident(w_up.shape, const2), _resident(conv_w.shape, const2),
                _resident((1, 2 * d_ff), const2), _resident(w_down.shape, const2),
                pl.BlockSpec((1,) + prev.shape[1:], per_batch)]
    out_specs = (pl.BlockSpec((1, tile, d_model), row), pl.BlockSpec((1, SUBLANES, 2 * d_ff), per_batch))
    out_shape = (jax.ShapeDtypeStruct((bsz, seq, d_model), _F32),
                 jax.ShapeDtypeStruct((bsz, SUBLANES, 2 * d_ff), _F32))
    kern = functools.partial(_ffn_kernel, d_ff=d_ff, carried=carried)
    return pl.pallas_call(
        kern, grid=(bsz, n_tiles), in_specs=in_specs, out_specs=out_specs, out_shape=out_shape,
        scratch_shapes=[pltpu.VMEM((SUBLANES, 2 * d_ff), _F32), pltpu.VMEM((tile, d_ff), _BF16)],
        compiler_params=_params(2), name="ffn")(x, mixed, w_out, n2, w_up, conv_w, conv_b, w_down, prev)


def _rel_rows(table):
    n_low = B_REACH - REL_CLIP + ATTN_SUB - 1
    n_high = REL_ROW - n_low - table.shape[0]
    rows = jnp.concatenate([jnp.broadcast_to(table[-1:], (n_low, table.shape[1])),
                            table[::-1],
                            jnp.broadcast_to(table[:1], (n_high, table.shape[1]))], axis=0)
    return rows.T


def _tile2(g, scale=1.0):
    return (jnp.tile(g, 2) * scale).reshape(1, LANES)


def _rows_major(t):
    b, ns, r, _ = t.shape
    return jnp.swapaxes(t, 1, 2).reshape(b, r, ns * LANES)


def _slab_major(t, twice=False):
    b, r, c = t.shape
    if twice:
        t = jnp.repeat(t.reshape(b, r, c // HEAD_DIM, 1, HEAD_DIM), 2, axis=3).reshape(b, r, 2 * c)
    return jnp.swapaxes(t.reshape(b, r, -1, LANES), 1, 2)


def _split_batches(t, nb):
    _, ns, rows, _ = t.shape
    return jnp.swapaxes(t.reshape(ns, nb, rows // nb, LANES), 0, 1)


def kernel(x_prompt, x_sample, cache_a_k, cache_a_v, cache_b_k, cache_b_v, cache_ffn_conv, norm1_g, w_in, b_gate, qn_a_g, kn_a_g, qn_b_g, kn_b_g, sinks_a, rel_bias_b, w_out, norm2_g, w_up, conv_w, conv_b, w_down):
    depth = w_in.shape[0]
    bsz, seq, d_model = x_prompt.shape
    dbs, dseq, _ = x_sample.shape
    n_heads = d_model // HEAD_DIM
    kv_heads = n_heads // A_GROUP
    two_ff = w_up.shape[-1]
    assert cache_a_k.shape[2] == A_REACH and cache_b_k.shape[2] == B_REACH
    assert seq % PROJ_TILE == 0 and PROJ_TILE == B_REACH and dseq <= CHUNK

    slopes = 2.0 ** (-8.0 * jnp.arange(1, n_heads + 1, dtype=_F32) / n_heads)
    xp, xs = x_prompt, x_sample.reshape(1, dbs * dseq, d_model)
    zero_prev = jnp.zeros((bsz, CONV_W - 1, two_ff), _F32)
    heads = lambda t, n: _rows_major(t).reshape(t.shape[0], t.shape[2], n, HEAD_DIM)
    pk, sk = [], []
    for l in range(depth):
        w_in_l, w_out_l = w_in[l].astype(_BF16), w_out[l].astype(_BF16)
        w_up_l, w_down_l = w_up[l].astype(_BF16), w_down[l].astype(_BF16)
        n1, n2 = norm1_g[l].reshape(1, -1), norm2_g[l].reshape(1, -1)
        bg, cb = b_gate[l].reshape(1, -1), conv_b[l].reshape(1, -1)
        gains = (_tile2(qn_a_g[l], Q_SCALE), _tile2(kn_a_g[l]), _tile2(qn_b_g[l], Q_SCALE), _tile2(kn_b_g[l]))
        rel = _rel_rows(rel_bias_b[l])

        qa, qb, g, ka, va, kb, vb, kat, vat, kbt, vbt = _proj(
            xp, n1, w_in_l, bg, *gains, tile=PROJ_TILE, n_pad=1)
        mixed = _attention(slopes, sinks_a[l], rel, qa, qb, g, ka, va, kb, vb, tq=ATTN_TILE, windowed=True)
        xp, st = _ffn(xp, mixed, w_out_l, n2, w_up_l, conv_w[l], cb, w_down_l, zero_prev,
                      tile=PROJ_TILE, carried=True)
        pk.append((heads(kat, kv_heads)[:, -A_REACH:], heads(vat, kv_heads)[:, -A_REACH:],
                   heads(kbt, n_heads), heads(vbt, n_heads), st[:, -(CONV_W - 1):]))

        qa, qb, g, ka, va, kb, vb, kat, vat, kbt, vbt = _proj(
            xs, n1, w_in_l, bg, *gains, tile=dbs * dseq, n_pad=0)

        def win(cache, new, twice=False):
            rows = _window(cache.shape[1], dseq)
            old = _slab_major(cache.reshape(dbs, cache.shape[1], -1).astype(_BF16), twice)
            w = jnp.concatenate([old, _split_batches(new, dbs)], axis=2)
            return jnp.pad(w, ((0, 0), (0, 0), (0, rows - w.shape[2]), (0, 0)))

        mixed = _attention(slopes, sinks_a[l], rel, _split_batches(qa, dbs), _split_batches(qb, dbs),
                           _split_batches(g, dbs),
                           win(cache_a_k[l], ka, True), win(cache_a_v[l], va, True),
                           win(cache_b_k[l], kb), win(cache_b_v[l], vb), tq=dseq, windowed=False)
        ys, st = _ffn(xs.reshape(dbs, dseq, d_model), mixed, w_out_l, n2, w_up_l, conv_w[l], cb, w_down_l,
                      cache_ffn_conv[l], tile=dseq, carried=False)
        xs = ys.reshape(1, dbs * dseq, d_model)
        per_b = lambda t, n: heads(t, n).reshape(dbs, dseq, n, HEAD_DIM)
        sk.append((per_b(kat, kv_heads), per_b(vat, kv_heads), per_b(kbt, n_heads), per_b(vbt, n_heads),
                   st[:, -(CONV_W - 1):]))

    stk = lambda states, i: jnp.stack([s[i] for s in states])
    return (xp, xs.reshape(dbs, dseq, d_model),
            stk(pk, 0), stk(pk, 1), stk(pk, 2), stk(pk, 3), stk(pk, 4),
            stk(sk, 0), stk(sk, 1), stk(sk, 2), stk(sk, 3), stk(sk, 4))
```

```python
import functools

import jax
import jax.numpy as jnp
from jax import lax
from jax.experimental import pallas as pl
from jax.experimental.pallas import tpu as pltpu

HEAD_DIM = 64
CHUNK = 64
A_GROUP = 4
A_PREV = 2
B_PREV = 8
A_REACH = A_PREV * CHUNK
B_REACH = B_PREV * CHUNK
REL_CLIP = 128
CONV_W = 3
EPS = 1e-6
NEG_INF = -1e30
Q_SCALE = HEAD_DIM ** -0.5

LANES = 128
SUBLANES = 8
MXU_WIDTH = 256
VMEM_LIMIT_BYTES = 56 * 1024 * 1024

PROJ_TILE = 512
ATTN_TILE = 256
ATTN_SUB = 128
FF_CHUNK = 256
REL_ROW = 1024

_BF16 = jnp.bfloat16
_F32 = jnp.float32


def _resident(shape, index_map):
    return pl.BlockSpec(shape, index_map, pipeline_mode=pl.Buffered(1))


def _params(n_axes, flags=None):
    return pltpu.CompilerParams(dimension_semantics=("arbitrary",) * n_axes,
                                vmem_limit_bytes=VMEM_LIMIT_BYTES, flags=flags)


def _rmsnorm_rows(x, g):
    return x * lax.rsqrt(jnp.mean(x * x, axis=-1, keepdims=True) + EPS) * g


def _lane_is_low():
    return lax.broadcasted_iota(jnp.int32, (1, LANES), 1) < HEAD_DIM


def _headnorm_slab(z, g2):
    low = _lane_is_low()
    sq = z * z
    s_lo = jnp.sum(jnp.where(low, sq, 0.0), axis=-1, keepdims=True)
    s_hi = jnp.sum(jnp.where(low, 0.0, sq), axis=-1, keepdims=True)
    ms = jnp.where(low, s_lo, s_hi) * (1.0 / HEAD_DIM)
    return z * lax.rsqrt(ms + EPS) * g2


def _proj_kernel(x_ref, n1_ref, w_ref, bg_ref, gqa_ref, gka_ref, gqb_ref, gkb_ref,
                 qa_ref, qb_ref, g_ref, ka_ref, va_ref, kb_ref, vb_ref,
                 kat_ref, vat_ref, kbt_ref, vbt_ref, *, n_pad, d_model, a_kv):
    j = pl.program_id(1)
    o_ka = d_model
    o_va = o_ka + a_kv
    o_qb = o_va + a_kv
    o_kb = o_qb + d_model
    o_vb = o_kb + d_model
    o_g = o_vb + d_model

    if n_pad:
        @pl.when(j < n_pad)
        def _():
            for r in (ka_ref, va_ref, kb_ref, vb_ref):
                r[...] = jnp.zeros(r.shape, r.dtype)

    @pl.when(j >= n_pad)
    def _():
        h = _rmsnorm_rows(x_ref[0], n1_ref[...]).astype(_BF16)
        low = _lane_is_low()

        def slabs(base, width):
            for c in range(0, width, MXU_WIDTH):
                z = jnp.dot(h, w_ref[:, base + c:base + c + MXU_WIDTH], preferred_element_type=_F32)
                for t in range(MXU_WIDTH // LANES):
                    yield c // LANES + t, z[:, t * LANES:(t + 1) * LANES]

        def emit(out_ref, base, width, gain_ref=None, tail_ref=None, twice=False, feature_major=False):
            def put(idx, y):
                out_ref[0, idx] = (y.T if feature_major else y).astype(out_ref.dtype)

            for s, y in slabs(base, width):
                if gain_ref is not None:
                    y = _headnorm_slab(y, gain_ref[...])
                if tail_ref is not None:
                    tail_ref[0, s] = y
                if twice:
                    y_sw = pltpu.roll(y, HEAD_DIM, 1)
                    put(2 * s, jnp.where(low, y, y_sw))
                    put(2 * s + 1, jnp.where(low, y_sw, y))
                else:
                    put(s, y)

        emit(qa_ref, 0, d_model, gqa_ref)
        emit(ka_ref, o_ka, a_kv, gka_ref, kat_ref, twice=True, feature_major=True)
        emit(va_ref, o_va, a_kv, None, vat_ref, twice=True)
        emit(qb_ref, o_qb, d_model, gqb_ref)
        emit(kb_ref, o_kb, d_model, gkb_ref, kbt_ref, feature_major=True)
        emit(vb_ref, o_vb, d_model, None, vbt_ref)
        for s, y in slabs(o_g, 2 * d_model):
            c = s * LANES
            g_ref[0, s] = jax.nn.sigmoid(y + bg_ref[:, c:c + LANES]).astype(g_ref.dtype)


def _proj(x, n1, w_in, b_gate, gqa, gka, gqb, gkb, *, tile, n_pad):
    bsz, seq, d_model = x.shape
    a_kv = d_model // A_GROUP
    n_tiles = seq // tile
    n_steps = n_tiles + n_pad
    pad_rows = n_pad * tile
    ns_d, ns_kv = d_model // LANES, a_kv // LANES

    def row(b, j):
        return (b, jnp.maximum(j - n_pad, 0), 0)

    def slab_row(b, j):
        return (b, 0, jnp.maximum(j - n_pad, 0), 0)

    def slab_padded(b, j):
        return (b, 0, j, 0)

    def const2(b, j):
        return (0, 0)

    def tail(b, j):
        return (b, 0, 0, 0)

    def slab_padded_t(b, j):
        return (b, 0, 0, j)

    bf = lambda ns, rows: jax.ShapeDtypeStruct((bsz, ns, rows, LANES), _BF16)
    bf_t = lambda ns, rows: jax.ShapeDtypeStruct((bsz, ns, LANES, rows), _BF16)
    f32 = lambda ns: jax.ShapeDtypeStruct((bsz, ns, tile, LANES), _F32)
    out_shape = (bf(ns_d, seq), bf(ns_d, seq), bf(2 * ns_d, seq),
                 bf_t(2 * ns_kv, pad_rows + seq), bf(2 * ns_kv, pad_rows + seq),
                 bf_t(ns_d, pad_rows + seq), bf(ns_d, pad_rows + seq),
                 f32(ns_kv), f32(ns_kv), f32(ns_d), f32(ns_d))
    blk = lambda ns, imap: pl.BlockSpec((1, ns, tile, LANES), imap)
    blk_t = lambda ns: pl.BlockSpec((1, ns, LANES, tile), slab_padded_t)
    out_specs = (blk(ns_d, slab_row), blk(ns_d, slab_row), blk(2 * ns_d, slab_row),
                 blk_t(2 * ns_kv), blk(2 * ns_kv, slab_padded),
                 blk_t(ns_d), blk(ns_d, slab_padded),
                 blk(ns_kv, tail), blk(ns_kv, tail), blk(ns_d, tail), blk(ns_d, tail))
    in_specs = [pl.BlockSpec((1, tile, d_model), row),
                _resident((1, d_model), const2),
                _resident(w_in.shape, const2),
                _resident((1, 2 * d_model), const2),
                _resident((1, LANES), const2), _resident((1, LANES), const2),
                _resident((1, LANES), const2), _resident((1, LANES), const2)]
    kern = functools.partial(_proj_kernel, n_pad=n_pad, d_model=d_model, a_kv=a_kv)
    return pl.pallas_call(kern, grid=(bsz, n_steps), in_specs=in_specs, out_specs=out_specs,
                          out_shape=out_shape, compiler_params=_params(2), name="proj")(
                              x, n1, w_in, b_gate, gqa, gka, gqb, gkb)


def _band_mask(tq, w, n_prev):
    shift = CHUNK.bit_length() - 1
    qc = lax.shift_right_logical(lax.broadcasted_iota(jnp.int32, (tq, w), 0), shift)
    kc = lax.shift_right_logical(lax.broadcasted_iota(jnp.int32, (tq, w), 1), shift)
    d = kc - qc
    return (d >= 0) & (d <= n_prev)


def _window(reach, tq):
    return -(-(reach + tq) // LANES) * LANES


def _attn_kernel(slopes_ref, sinks_ref, rel_ref, qa_ref, qb_ref, g_ref,
                 ka_ref, va_ref, kb_ref, vb_ref, o_ref, bias_a, bias_b, *bufs,
                 tq, ts, wa, wb, windowed, n_slabs):
    b = pl.program_id(0)
    i = pl.program_id(1)
    sa_ref, sb_ref, pa_ref, pb_ref, da_ref = (bufs[2 * n:2 * n + 2] for n in range(5))

    @pl.when((b == 0) & (i == 0))
    def _():
        qpos = lax.broadcasted_iota(jnp.int32, (ts, wa), 0)
        kpos = lax.broadcasted_iota(jnp.int32, (ts, wa), 1) - A_REACH
        dist = jnp.abs(qpos - kpos).astype(_F32)
        band_a = _band_mask(ts, wa, A_PREV)
        band_b = _band_mask(ts, wb, B_PREV)
        for h in range(2 * n_slabs):
            rows_h = slice((h % 2) * ts, (h % 2 + 1) * ts)
            bias_a[h // 2, rows_h, :] = jnp.where(band_a, -slopes_ref[h] * dist, NEG_INF)
            rows = jnp.broadcast_to(rel_ref[h:h + 1, :], (ts, REL_ROW))
            rows = pltpu.roll(rows, REL_ROW - ATTN_SUB + 1, 1, stride=1, stride_axis=0)
            bias_b[h // 2, rows_h, :] = jnp.where(band_b, rows[:, :wb], NEG_INF)

    low = _lane_is_low()
    row_is_lo = lax.broadcasted_iota(jnp.int32, (2 * ts, 1), 0) < ts
    col_a = lax.broadcasted_iota(jnp.int32, (1, wa), 1)
    col_b = lax.broadcasted_iota(jnp.int32, (1, wb), 1)
    ones = jnp.ones((max(wa, wb), LANES), _BF16)

    def stacked(qs):
        zero = jnp.zeros_like(qs)
        return jnp.concatenate([jnp.where(low, qs, zero), jnp.where(low, zero, qs)], axis=0)

    def softmax_weights(s, sink_col):
        m = jnp.max(s, axis=-1, keepdims=True)
        if sink_col is not None:
            m = jnp.maximum(m, sink_col)
        p = jnp.exp(s - m).astype(_BF16)
        return p, None if sink_col is None else jnp.broadcast_to(jnp.exp(sink_col - m), (2 * ts, LANES))

    def weighted(p, v, extra_den):
        o2 = jnp.dot(p, jnp.concatenate([v, ones[:v.shape[0]]], axis=1), preferred_element_type=_F32)
        den = o2[:, LANES:]
        if extra_den is not None:
            den = den + extra_den
        o = o2[:, :LANES] / den
        return jnp.where(low, o[:ts], o[ts:])

    def run(masked):
        subs = []
        for u in range(tq // ts):
            rows_u = slice(u * ts, (u + 1) * ts)
            if windowed:
                start = pl.multiple_of(i * tq, tq) + u * ts
                valid = (jnp.where(col_a >= A_REACH - start, 0.0, NEG_INF),
                         jnp.where(col_b >= B_REACH - start, 0.0, NEG_INF)) if masked else (None, None)
                subs.append((rows_u, start + (B_REACH - A_REACH), start) + valid)
            else:
                subs.append((rows_u, 0, 0, jnp.where(col_a < A_REACH + tq, 0.0, NEG_INF),
                             jnp.where(col_b < B_REACH + tq, 0.0, NEG_INF)))

        def scores(k, u, par):
            rows_u, sa, sb, _, _ = subs[u]
            sa_ref[par][...] = jnp.dot(stacked(qa_ref[0, k, rows_u]), ka_ref[0, k // 2, :, pl.ds(sa, wa)],
                                       preferred_element_type=_F32)
            sb_ref[par][...] = jnp.dot(stacked(qb_ref[0, k, rows_u]), kb_ref[0, k, :, pl.ds(sb, wb)],
                                       preferred_element_type=_F32)

        def softmax(k, u, par):
            _, _, _, valid_a, valid_b = subs[u]
            sink_col = jnp.where(row_is_lo, sinks_ref[2 * k], sinks_ref[2 * k + 1])
            s = sa_ref[par][...] + bias_a[k]
            pa_ref[par][...], da_ref[par][...] = softmax_weights(s if valid_a is None else s + valid_a, sink_col)
            s = sb_ref[par][...] + bias_b[k]
            pb_ref[par][...], _ = softmax_weights(s if valid_b is None else s + valid_b, None)

        def values(k, u, par):
            rows_u, sa, sb, _, _ = subs[u]
            oa = weighted(pa_ref[par][...], va_ref[0, k // 2, pl.ds(sa, wa)], da_ref[par][...])
            ob = weighted(pb_ref[par][...], vb_ref[0, k, pl.ds(sb, wb)], None)
            mixed = g_ref[0, k, rows_u].astype(_F32) * oa + g_ref[0, n_slabs + k, rows_u].astype(_F32) * ob
            o_ref[0, k, rows_u] = mixed.astype(o_ref.dtype)

        n_sub = len(subs)
        if n_sub == 2:
            scores(0, 0, 0)
            softmax(0, 0, 0)
            scores(0, 1, 1)

            def body(k, carry):
                values(k - 1, 0, 0)
                softmax(k - 1, 1, 1)
                scores(k, 0, 0)
                values(k - 1, 1, 1)
                softmax(k, 0, 0)
                scores(k, 1, 1)
                return carry

            lax.fori_loop(1, n_slabs, body, 0)
            values(n_slabs - 1, 0, 0)
            softmax(n_slabs - 1, 1, 1)
            values(n_slabs - 1, 1, 1)
        else:
            scores(0, 0, 0)
            softmax(0, 0, 0)
            scores(1, 0, 1)

            def body(j, carry):
                k = 2 * j + 2
                values(k - 2, 0, 0)
                softmax(k - 1, 0, 1)
                scores(k, 0, 0)
                values(k - 1, 0, 1)
                softmax(k, 0, 0)
                scores(k + 1, 0, 1)
                return carry

            lax.fori_loop(0, (n_slabs - 2) // 2, body, 0)
            values(n_slabs - 2, 0, 0)
            softmax(n_slabs - 1, 0, 1)
            values(n_slabs - 1, 0, 1)

    if windowed:
        n_early = B_REACH // tq
        pl.when(i < n_early)(lambda: run(True))
        pl.when(i >= n_early)(lambda: run(False))
    else:
        run(True)


def _attention(slopes, sinks, rel_rows, qa, qb, gates, ka, va, kb, vb, *, tq, windowed):
    bsz, n_slabs, seq, _ = qa.shape
    ts = min(tq, ATTN_SUB)
    wa = _window(A_REACH, ts)
    wb = _window(B_REACH, ts)
    n_tiles = seq // tq

    def row(b, i):
        return (b, 0, i, 0)

    def per_batch(b, i):
        return (b, 0, 0, 0)

    def const2(b, i):
        return (0, 0)

    smem = pl.BlockSpec(memory_space=pltpu.SMEM)
    rows = lambda ns: pl.BlockSpec((1, ns, tq, LANES), row)
    whole = lambda t: _resident((1,) + t.shape[1:], per_batch)
    in_specs = [smem, smem, _resident(rel_rows.shape, const2),
                rows(n_slabs), rows(n_slabs), rows(2 * n_slabs),
                whole(ka), whole(va), whole(kb), whole(vb)]
    kern = functools.partial(_attn_kernel, tq=tq, ts=ts, wa=wa, wb=wb, windowed=windowed, n_slabs=n_slabs)
    return pl.pallas_call(
        kern, grid=(bsz, n_tiles), in_specs=in_specs, out_specs=rows(n_slabs),
        out_shape=jax.ShapeDtypeStruct(qa.shape, _BF16),
        scratch_shapes=[pltpu.VMEM((n_slabs, 2 * ts, wa), _F32), pltpu.VMEM((n_slabs, 2 * ts, wb), _F32)]
        + [pltpu.VMEM((2 * ts, w), dt)
           for w, dt in ((wa, _F32), (wb, _F32), (wa, _BF16), (wb, _BF16), (LANES, _F32)) for _ in range(2)],
        compiler_params=_params(2), name="attn")(slopes, sinks, rel_rows, qa, qb, gates, ka, va, kb, vb)


def _gelu(x):
    return 0.5 * x * (1.0 + lax.erf(x * (2.0 ** -0.5)))


def _ffn_kernel(x_ref, m_ref, wo_ref, n2_ref, wu_ref, cw_ref, cb_ref, wd_ref, prev_ref,
                y_ref, st_ref, carry, act, *, d_ff, carried):
    i = pl.program_id(1)
    rows = x_ref.shape[1]

    if carried:
        @pl.when(i == 0)
        def _():
            carry[...] = jnp.zeros(carry.shape, carry.dtype)

    mixed = jnp.concatenate([m_ref[0, s] for s in range(m_ref.shape[1])], axis=-1)
    x1 = x_ref[0] + jnp.dot(mixed, wo_ref[...], preferred_element_type=_F32)
    h = _rmsnorm_rows(x1, n2_ref[...]).astype(_BF16)
    r = lax.broadcasted_iota(jnp.int32, (rows, 1), 0)

    def conv(col):
        u = jnp.dot(h, wu_ref[:, col:col + FF_CHUNK], preferred_element_type=_F32)
        cs = slice(col, col + FF_CHUNK)
        if carried:
            p0, p1 = carry[SUBLANES - 2:SUBLANES - 1, cs], carry[SUBLANES - 1:SUBLANES, cs]
            carry[:, cs] = u[rows - SUBLANES:, :]
        else:
            p0, p1 = prev_ref[0, 0:1, cs], prev_ref[0, 1:2, cs]
        st_ref[0, :, cs] = u[rows - SUBLANES:, :]
        u1 = jnp.where(r == 0, p1, pltpu.roll(u, 1, 0))
        u2 = jnp.where(r == 0, p0, jnp.where(r == 1, p1, pltpu.roll(u, 2, 0)))
        return cb_ref[:, cs] + cw_ref[0:1, cs] * u2 + cw_ref[1:2, cs] * u1 + cw_ref[2:3, cs] * u

    for c in range(0, d_ff, FF_CHUNK):
        a = conv(c)
        g = conv(d_ff + c)
        act[:, c:c + FF_CHUNK] = (_gelu(a) * g).astype(act.dtype)

    y_ref[0] = x1 + jnp.dot(act[...], wd_ref[...], preferred_element_type=_F32)


def _ffn(x, mixed, w_out, n2, w_up, conv_w, conv_b, w_down, prev, *, tile, carried):
    bsz, seq, d_model = x.shape
    d_ff = w_down.shape[0]
    n_tiles = seq // tile

    def row(b, i):
        return (b, i, 0)

    def slab_row(b, i):
        return (b, 0, i, 0)

    def per_batch(b, i):
        return (b, 0, 0)

    def const2(b, i):
        return (0, 0)

    in_specs = [pl.BlockSpec((1, tile, d_model), row),
                pl.BlockSpec((1, mixed.shape[1], tile, LANES), slab_row),
                _resident(w_out.shape, const2), _resident((1, d_model), const2),
                _resident(w_up.shape, const2), _resident(conv_w.shape, const2),
                _resident((1, 2 * d_ff), const2), _resident(w_down.shape, const2),
                pl.BlockSpec((1,) + prev.shape[1:], per_batch)]
    out_specs = (pl.BlockSpec((1, tile, d_model), row), pl.BlockSpec((1, SUBLANES, 2 * d_ff), per_batch))
    out_shape = (jax.ShapeDtypeStruct((bsz, seq, d_model), _F32),
                 jax.ShapeDtypeStruct((bsz, SUBLANES, 2 * d_ff), _F32))
    kern = functools.partial(_ffn_kernel, d_ff=d_ff, carried=carried)
    return pl.pallas_call(
        kern, grid=(bsz, n_tiles), in_specs=in_specs, out_specs=out_specs, out_shape=out_shape,
        scratch_shapes=[pltpu.VMEM((SUBLANES, 2 * d_ff), _F32), pltpu.VMEM((tile, d_ff), _BF16)],
        compiler_params=_params(2), name="ffn")(x, mixed, w_out, n2, w_up, conv_w, conv_b, w_down, prev)


def _rel_rows(table):
    n_low = B_REACH - REL_CLIP + ATTN_SUB - 1
    n_high = REL_ROW - n_low - table.shape[0]
    rows = jnp.concatenate([jnp.broadcast_to(table[-1:], (n_low, table.shape[1])),
                            table[::-1],
                            jnp.broadcast_to(table[:1], (n_high, table.shape[1]))], axis=0)
    return rows.T


def _tile2(g, scale=1.0):
    return (jnp.tile(g, 2) * scale).reshape(1, LANES)


def _rows_major(t):
    b, ns, r, _ = t.shape
    return jnp.swapaxes(t, 1, 2).reshape(b, r, ns * LANES)


def _slab_major(t, twice=False):
    b, r, c = t.shape
    if twice:
        t = jnp.repeat(t.reshape(b, r, c // HEAD_DIM, 1, HEAD_DIM), 2, axis=3).reshape(b, r, 2 * c)
    return jnp.swapaxes(t.reshape(b, r, -1, LANES), 1, 2)


def _split_batches(t, nb):
    _, ns, rows, _ = t.shape
    return jnp.swapaxes(t.reshape(ns, nb, rows // nb, LANES), 0, 1)


def kernel(x_prompt, x_sample, cache_a_k, cache_a_v, cache_b_k, cache_b_v, cache_ffn_conv, norm1_g, w_in, b_gate, qn_a_g, kn_a_g, qn_b_g, kn_b_g, sinks_a, rel_bias_b, w_out, norm2_g, w_up, conv_w, conv_b, w_down):
    depth = w_in.shape[0]
    bsz, seq, d_model = x_prompt.shape
    dbs, dseq, _ = x_sample.shape
    n_heads = d_model // HEAD_DIM
    kv_heads = n_heads // A_GROUP
    two_ff = w_up.shape[-1]
    assert cache_a_k.shape[2] == A_REACH and cache_b_k.shape[2] == B_REACH
    assert seq % PROJ_TILE == 0 and PROJ_TILE == B_REACH and dseq <= CHUNK

    slopes = 2.0 ** (-8.0 * jnp.arange(1, n_heads + 1, dtype=_F32) / n_heads)
    xp, xs = x_prompt, x_sample.reshape(1, dbs * dseq, d_model)
    zero_prev = jnp.zeros((bsz, CONV_W - 1, two_ff), _F32)
    heads = lambda t, n: _rows_major(t).reshape(t.shape[0], t.shape[2], n, HEAD_DIM)
    pk, sk = [], []
    for l in range(depth):
        w_in_l, w_out_l = w_in[l].astype(_BF16), w_out[l].astype(_BF16)
        w_up_l, w_down_l = w_up[l].astype(_BF16), w_down[l].astype(_BF16)
        n1, n2 = norm1_g[l].reshape(1, -1), norm2_g[l].reshape(1, -1)
        bg, cb = b_gate[l].reshape(1, -1), conv_b[l].reshape(1, -1)
        gains = (_tile2(qn_a_g[l], Q_SCALE), _tile2(kn_a_g[l]), _tile2(qn_b_g[l], Q_SCALE), _tile2(kn_b_g[l]))
        rel = _rel_rows(rel_bias_b[l])

        qa, qb, g, ka, va, kb, vb, kat, vat, kbt, vbt = _proj(
            xp, n1, w_in_l, bg, *gains, tile=PROJ_TILE, n_pad=1)
        mixed = _attention(slopes, sinks_a[l], rel, qa, qb, g, ka, va, kb, vb, tq=ATTN_TILE, windowed=True)
        xp, st = _ffn(xp, mixed, w_out_l, n2, w_up_l, conv_w[l], cb, w_down_l, zero_prev,
                      tile=PROJ_TILE, carried=True)
        pk.append((heads(kat, kv_heads)[:, -A_REACH:], heads(vat, kv_heads)[:, -A_REACH:],
                   heads(kbt, n_heads), heads(vbt, n_heads), st[:, -(CONV_W - 1):]))

        qa, qb, g, ka, va, kb, vb, kat, vat, kbt, vbt = _proj(
            xs, n1, w_in_l, bg, *gains, tile=dbs * dseq, n_pad=0)

        def win(cache, new, twice=False, feature_major=False):
            rows = _window(cache.shape[1], dseq)
            old = _slab_major(cache.reshape(dbs, cache.shape[1], -1).astype(_BF16), twice)
            if feature_major:
                new = jnp.swapaxes(new, 2, 3)
            w = jnp.concatenate([old, _split_batches(new, dbs)], axis=2)
            w = jnp.pad(w, ((0, 0), (0, 0), (0, rows - w.shape[2]), (0, 0)))
            return jnp.swapaxes(w, 2, 3) if feature_major else w

        mixed = _attention(slopes, sinks_a[l], rel, _split_batches(qa, dbs), _split_batches(qb, dbs),
                           _split_batches(g, dbs),
                           win(cache_a_k[l], ka, True, True), win(cache_a_v[l], va, True),
                           win(cache_b_k[l], kb, False, True), win(cache_b_v[l], vb), tq=dseq, windowed=False)
        ys, st = _ffn(xs.reshape(dbs, dseq, d_model), mixed, w_out_l, n2, w_up_l, conv_w[l], cb, w_down_l,
                      cache_ffn_conv[l], tile=dseq, carried=False)
        xs = ys.reshape(1, dbs * dseq, d_model)
        per_b = lambda t, n: heads(t, n).reshape(dbs, dseq, n, HEAD_DIM)
        sk.append((per_b(kat, kv_heads), per_b(vat, kv_heads), per_b(kbt, n_heads), per_b(vbt, n_heads),
                   st[:, -(CONV_W - 1):]))

    stk = lambda states, i: jnp.stack([s[i] for s in states])
    return (xp, xs.reshape(dbs, dseq, d_model),
            stk(pk, 0), stk(pk, 1), stk(pk, 2), stk(pk, 3), stk(pk, 4),
            stk(sk, 0), stk(sk, 1), stk(sk, 2), stk(sk, 3), stk(sk, 4))
```

```python
import functools

import jax
import jax.numpy as jnp
from jax import lax
from jax.experimental import pallas as pl
from jax.experimental.pallas import tpu as pltpu

HEAD_DIM = 64
CHUNK = 64
A_GROUP = 4
A_PREV = 2
B_PREV = 8
A_REACH = A_PREV * CHUNK
B_REACH = B_PREV * CHUNK
REL_CLIP = 128
CONV_W = 3
EPS = 1e-6
NEG_INF = -1e30
LOG2E = 1.4426950408889634
Q_SCALE = HEAD_DIM ** -0.5 * LOG2E

LANES = 128
SUBLANES = 8
MXU_WIDTH = 256
VMEM_LIMIT_BYTES = 56 * 1024 * 1024

PROJ_TILE = 512
ATTN_TILE = 256
ATTN_SUB = 128
FF_CHUNK = 256
REL_ROW = 1024

_BF16 = jnp.bfloat16
_F32 = jnp.float32


def _resident(shape, index_map):
    return pl.BlockSpec(shape, index_map, pipeline_mode=pl.Buffered(1))


def _layer_resident(w, layer):
    return pl.BlockSpec((None,) + w.shape[1:], lambda *_: (layer, 0, 0), pipeline_mode=pl.Buffered(1))


def _params(n_axes, flags=None):
    return pltpu.CompilerParams(dimension_semantics=("arbitrary",) * n_axes,
                                vmem_limit_bytes=VMEM_LIMIT_BYTES, flags=flags)


def _rmsnorm_rows(x, g):
    return x * lax.rsqrt(jnp.mean(x * x, axis=-1, keepdims=True) + EPS) * g


def _lane_is_low():
    return lax.broadcasted_iota(jnp.int32, (1, LANES), 1) < HEAD_DIM


def _headnorm_slab(z, g2):
    low = _lane_is_low()
    sq = z * z
    s_lo = jnp.sum(jnp.where(low, sq, 0.0), axis=-1, keepdims=True)
    s_hi = jnp.sum(jnp.where(low, 0.0, sq), axis=-1, keepdims=True)
    ms = jnp.where(low, s_lo, s_hi) * (1.0 / HEAD_DIM)
    return z * lax.rsqrt(ms + EPS) * g2


CAST_BLOCK_BYTES = 6 * 1024 * 1024


def _cast_kernel(x_ref, o_ref):
    o_ref[...] = x_ref[...].astype(o_ref.dtype)


def _to_bf16(w):
    depth, rows, cols = w.shape
    packing = 2 * SUBLANES
    fits = [r for r in range(packing, rows + 1, packing) if rows % r == 0 and r * cols * 4 <= CAST_BLOCK_BYTES]
    rb = max(fits)
    spec = pl.BlockSpec((1, rb, cols), lambda l, i: (l, i, 0))
    return pl.pallas_call(_cast_kernel, grid=(depth, rows // rb), in_specs=[spec], out_specs=spec,
                          out_shape=jax.ShapeDtypeStruct(w.shape, _BF16), compiler_params=_params(2),
                          name="cast")(w)


def _proj_kernel(x_ref, n1_ref, w_ref, bg_ref, gqa_ref, gka_ref, gqb_ref, gkb_ref,
                 qa_ref, qb_ref, g_ref, ka_ref, va_ref, kb_ref, vb_ref,
                 kat_ref, vat_ref, kbt_ref, vbt_ref, *, n_pad, d_model, a_kv):
    j = pl.program_id(1)
    o_ka = d_model
    o_va = o_ka + a_kv
    o_qb = o_va + a_kv
    o_kb = o_qb + d_model
    o_vb = o_kb + d_model
    o_g = o_vb + d_model

    if n_pad:
        @pl.when(j < n_pad)
        def _():
            for r in (ka_ref, va_ref, kb_ref, vb_ref):
                r[...] = jnp.zeros(r.shape, r.dtype)

    @pl.when(j >= n_pad)
    def _():
        h = _rmsnorm_rows(x_ref[0], n1_ref[...]).astype(_BF16)
        low = _lane_is_low()

        def slabs(base, width):
            for c in range(0, width, MXU_WIDTH):
                z = jnp.dot(h, w_ref[:, base + c:base + c + MXU_WIDTH], preferred_element_type=_F32)
                for t in range(MXU_WIDTH // LANES):
                    yield c // LANES + t, z[:, t * LANES:(t + 1) * LANES]

        def emit(out_ref, base, width, gain_ref=None, tail_ref=None, twice=False, feature_major=False):
            def put(idx, y):
                out_ref[0, idx] = (y.T if feature_major else y).astype(out_ref.dtype)

            for s, y in slabs(base, width):
                if gain_ref is not None:
                    y = _headnorm_slab(y, gain_ref[...])
                if tail_ref is not None:
                    tail_ref[0, :, s * LANES:(s + 1) * LANES] = y
                if twice:
                    y_sw = pltpu.roll(y, HEAD_DIM, 1)
                    put(2 * s, jnp.where(low, y, y_sw))
                    put(2 * s + 1, jnp.where(low, y_sw, y))
                else:
                    put(s, y)

        emit(qa_ref, 0, d_model, gqa_ref)
        emit(ka_ref, o_ka, a_kv, gka_ref, kat_ref, twice=True, feature_major=True)
        emit(va_ref, o_va, a_kv, None, vat_ref, twice=True)
        emit(qb_ref, o_qb, d_model, gqb_ref)
        emit(kb_ref, o_kb, d_model, gkb_ref, kbt_ref, feature_major=True)
        emit(vb_ref, o_vb, d_model, None, vbt_ref)
        for s, y in slabs(o_g, 2 * d_model):
            c = s * LANES
            g_ref[0, s] = jax.nn.sigmoid(y + bg_ref[:, c:c + LANES]).astype(g_ref.dtype)


def _proj(x, n1, w_in, b_gate, gqa, gka, gqb, gkb, *, layer, tile, n_pad):
    bsz, seq, d_model = x.shape
    a_kv = d_model // A_GROUP
    n_tiles = seq // tile
    n_steps = n_tiles + n_pad
    pad_rows = n_pad * tile
    ns_d, ns_kv = d_model // LANES, a_kv // LANES

    def row(b, j):
        return (b, jnp.maximum(j - n_pad, 0), 0)

    def slab_row(b, j):
        return (b, 0, jnp.maximum(j - n_pad, 0), 0)

    def slab_padded(b, j):
        return (b, 0, j, 0)

    def const2(b, j):
        return (0, 0)

    def tail(b, j):
        return (b, 0, 0)

    def slab_padded_t(b, j):
        return (b, 0, 0, j)

    bf = lambda ns, rows: jax.ShapeDtypeStruct((bsz, ns, rows, LANES), _BF16)
    bf_t = lambda ns, rows: jax.ShapeDtypeStruct((bsz, ns, LANES, rows), _BF16)
    f32 = lambda ns: jax.ShapeDtypeStruct((bsz, tile, ns * LANES), _F32)
    out_shape = (bf(ns_d, seq), bf(ns_d, seq), bf(2 * ns_d, seq),
                 bf_t(2 * ns_kv, pad_rows + seq), bf(2 * ns_kv, pad_rows + seq),
                 bf_t(ns_d, pad_rows + seq), bf(ns_d, pad_rows + seq),
                 f32(ns_kv), f32(ns_kv), f32(ns_d), f32(ns_d))
    blk = lambda ns, imap: pl.BlockSpec((1, ns, tile, LANES), imap)
    blk_t = lambda ns: pl.BlockSpec((1, ns, LANES, tile), slab_padded_t)
    blk_tail = lambda ns: pl.BlockSpec((1, tile, ns * LANES), tail)
    out_specs = (blk(ns_d, slab_row), blk(ns_d, slab_row), blk(2 * ns_d, slab_row),
                 blk_t(2 * ns_kv), blk(2 * ns_kv, slab_padded),
                 blk_t(ns_d), blk(ns_d, slab_padded),
                 blk_tail(ns_kv), blk_tail(ns_kv), blk_tail(ns_d), blk_tail(ns_d))
    in_specs = [pl.BlockSpec((1, tile, d_model), row),
                _resident((1, d_model), const2),
                _layer_resident(w_in, layer),
                _resident((1, 2 * d_model), const2),
                _resident((1, LANES), const2), _resident((1, LANES), const2),
                _resident((1, LANES), const2), _resident((1, LANES), const2)]
    kern = functools.partial(_proj_kernel, n_pad=n_pad, d_model=d_model, a_kv=a_kv)
    return pl.pallas_call(kern, grid=(bsz, n_steps), in_specs=in_specs, out_specs=out_specs,
                          out_shape=out_shape, compiler_params=_params(2), name="proj")(
                              x, n1, w_in, b_gate, gqa, gka, gqb, gkb)


def _band_mask(tq, w, n_prev):
    shift = CHUNK.bit_length() - 1
    qc = lax.shift_right_logical(lax.broadcasted_iota(jnp.int32, (tq, w), 0), shift)
    kc = lax.shift_right_logical(lax.broadcasted_iota(jnp.int32, (tq, w), 1), shift)
    d = kc - qc
    return (d >= 0) & (d <= n_prev)


def _window(reach, tq):
    return -(-(reach + tq) // LANES) * LANES


def _attn_kernel(slopes_ref, sinks_ref, rel_ref, qa_ref, qb_ref, g_ref,
                 ka_ref, va_ref, kb_ref, vb_ref, o_ref, bias_a, bias_b, *bufs,
                 tq, ts, wa, wb, windowed, n_slabs):
    b = pl.program_id(0)
    i = pl.program_id(1)
    sa_ref, sb_ref, pa_ref, pb_ref, da_ref = (bufs[2 * n:2 * n + 2] for n in range(5))

    @pl.when((b == 0) & (i == 0))
    def _():
        qpos = lax.broadcasted_iota(jnp.int32, (ts, wa), 0)
        kpos = lax.broadcasted_iota(jnp.int32, (ts, wa), 1) - A_REACH
        dist = jnp.abs(qpos - kpos).astype(_F32)
        band_a = _band_mask(ts, wa, A_PREV)
        band_b = _band_mask(ts, wb, B_PREV)
        for h in range(2 * n_slabs):
            rows_h = slice((h % 2) * ts, (h % 2 + 1) * ts)
            bias_a[h // 2, rows_h, :] = jnp.where(band_a, (-LOG2E * slopes_ref[h]) * dist, NEG_INF)
            rows = jnp.broadcast_to(rel_ref[h:h + 1, :], (ts, REL_ROW))
            rows = pltpu.roll(rows, REL_ROW - ATTN_SUB + 1, 1, stride=1, stride_axis=0)
            bias_b[h // 2, rows_h, :] = jnp.where(band_b, LOG2E * rows[:, :wb], NEG_INF)

    low = _lane_is_low()
    row_is_lo = lax.broadcasted_iota(jnp.int32, (2 * ts, 1), 0) < ts
    col_a = lax.broadcasted_iota(jnp.int32, (1, wa), 1)
    col_b = lax.broadcasted_iota(jnp.int32, (1, wb), 1)
    ones = jnp.ones((max(wa, wb), LANES), _BF16)

    def stacked(qs):
        zero = jnp.zeros_like(qs)
        return jnp.concatenate([jnp.where(low, qs, zero), jnp.where(low, zero, qs)], axis=0)

    def softmax_weights(s, sink_col):
        m = jnp.max(s, axis=-1, keepdims=True)
        if sink_col is not None:
            m = jnp.maximum(m, sink_col)
        p = jnp.exp2(s - m).astype(_BF16)
        return p, None if sink_col is None else jnp.broadcast_to(jnp.exp2(sink_col - m), (2 * ts, LANES))

    def weighted(p, v, extra_den):
        o2 = jnp.dot(p, jnp.concatenate([v, ones[:v.shape[0]]], axis=1), preferred_element_type=_F32)
        den = o2[:, LANES:]
        if extra_den is not None:
            den = den + extra_den
        o = o2[:, :LANES] / den
        return jnp.where(low, o[:ts], o[ts:])

    def run(masked):
        subs = []
        for u in range(tq // ts):
            rows_u = slice(u * ts, (u + 1) * ts)
            if windowed:
                start = pl.multiple_of(i * tq, tq) + u * ts
                valid = (jnp.where(col_a >= A_REACH - start, 0.0, NEG_INF),
                         jnp.where(col_b >= B_REACH - start, 0.0, NEG_INF)) if masked else (None, None)
                subs.append((rows_u, start + (B_REACH - A_REACH), start) + valid)
            else:
                subs.append((rows_u, 0, 0, jnp.where(col_a < A_REACH + tq, 0.0, NEG_INF),
                             jnp.where(col_b < B_REACH + tq, 0.0, NEG_INF)))

        def scores(k, u, par):
            rows_u, sa, sb, _, _ = subs[u]
            sa_ref[par][...] = jnp.dot(stacked(qa_ref[0, k, rows_u]), ka_ref[0, k // 2, :, pl.ds(sa, wa)],
                                       preferred_element_type=_F32)
            sb_ref[par][...] = jnp.dot(stacked(qb_ref[0, k, rows_u]), kb_ref[0, k, :, pl.ds(sb, wb)],
                                       preferred_element_type=_F32)

        def softmax(k, u, par):
            _, _, _, valid_a, valid_b = subs[u]
            sink_col = LOG2E * jnp.where(row_is_lo, sinks_ref[2 * k], sinks_ref[2 * k + 1])
            s = sa_ref[par][...] + bias_a[k]
            pa_ref[par][...], da_ref[par][...] = softmax_weights(s if valid_a is None else s + valid_a, sink_col)
            s = sb_ref[par][...] + bias_b[k]
            pb_ref[par][...], _ = softmax_weights(s if valid_b is None else s + valid_b, None)

        def values(k, u, par):
            rows_u, sa, sb, _, _ = subs[u]
            oa = weighted(pa_ref[par][...], va_ref[0, k // 2, pl.ds(sa, wa)], da_ref[par][...])
            ob = weighted(pb_ref[par][...], vb_ref[0, k, pl.ds(sb, wb)], None)
            mixed = g_ref[0, k, rows_u].astype(_F32) * oa + g_ref[0, n_slabs + k, rows_u].astype(_F32) * ob
            o_ref[0, k, rows_u] = mixed.astype(o_ref.dtype)

        n_sub = len(subs)
        if n_sub == 2:
            scores(0, 0, 0)
            softmax(0, 0, 0)
            scores(0, 1, 1)

            def body(k, carry):
                values(k - 1, 0, 0)
                softmax(k - 1, 1, 1)
                scores(k, 0, 0)
                values(k - 1, 1, 1)
                softmax(k, 0, 0)
                scores(k, 1, 1)
                return carry

            lax.fori_loop(1, n_slabs, body, 0)
            values(n_slabs - 1, 0, 0)
            softmax(n_slabs - 1, 1, 1)
            values(n_slabs - 1, 1, 1)
        else:
            scores(0, 0, 0)
            softmax(0, 0, 0)
            scores(1, 0, 1)

            def body(j, carry):
                k = 2 * j + 2
                values(k - 2, 0, 0)
                softmax(k - 1, 0, 1)
                scores(k, 0, 0)
                values(k - 1, 0, 1)
                softmax(k, 0, 0)
                scores(k + 1, 0, 1)
                return carry

            lax.fori_loop(0, (n_slabs - 2) // 2, body, 0)
            values(n_slabs - 2, 0, 0)
            softmax(n_slabs - 1, 0, 1)
            values(n_slabs - 1, 0, 1)

    if windowed:
        n_early = B_REACH // tq
        pl.when(i < n_early)(lambda: run(True))
        pl.when(i >= n_early)(lambda: run(False))
    else:
        run(True)


def _attention(slopes, sinks, rel_rows, qa, qb, gates, ka, va, kb, vb, *, tq, windowed):
    _, n_slabs, seq, _ = qa.shape
    bsz = ka.shape[0]
    ts = min(tq, ATTN_SUB)
    wa = _window(A_REACH, ts)
    wb = _window(B_REACH, ts)
    n_tiles = seq // tq if windowed else 1

    def row(b, i):
        return (b, 0, i, 0) if windowed else (0, 0, b, 0)

    def per_batch(b, i):
        return (b, 0, 0, 0)

    def const2(b, i):
        return (0, 0)

    smem = pl.BlockSpec(memory_space=pltpu.SMEM)
    rows = lambda ns: pl.BlockSpec((1, ns, tq, LANES), row)
    whole = lambda t: _resident((1,) + t.shape[1:], per_batch)
    in_specs = [smem, smem, _resident(rel_rows.shape, const2),
                rows(n_slabs), rows(n_slabs), rows(2 * n_slabs),
                whole(ka), whole(va), whole(kb), whole(vb)]
    kern = functools.partial(_attn_kernel, tq=tq, ts=ts, wa=wa, wb=wb, windowed=windowed, n_slabs=n_slabs)
    return pl.pallas_call(
        kern, grid=(bsz, n_tiles), in_specs=in_specs, out_specs=rows(n_slabs),
        out_shape=jax.ShapeDtypeStruct(qa.shape, _BF16),
        scratch_shapes=[pltpu.VMEM((n_slabs, 2 * ts, wa), _F32), pltpu.VMEM((n_slabs, 2 * ts, wb), _F32)]
        + [pltpu.VMEM((2 * ts, w), dt)
           for w, dt in ((wa, _F32), (wb, _F32), (wa, _BF16), (wb, _BF16), (LANES, _F32)) for _ in range(2)],
        compiler_params=_params(2), name="attn")(slopes, sinks, rel_rows, qa, qb, gates, ka, va, kb, vb)


def _gelu(x):
    return 0.5 * x * (1.0 + lax.erf(x * (2.0 ** -0.5)))


def _ffn_kernel(*refs, d_ff, seg, carried):
    if carried:
        x_ref, m_ref, wo_ref, n2_ref, wu_ref, cw_ref, cb_ref, wd_ref, y_ref, st_ref, carry, act = refs
    else:
        x_ref, m_ref, wo_ref, n2_ref, wu_ref, cw_ref, cb_ref, wd_ref, prev_ref, y_ref, st_ref, act = refs
    rows = x_ref.shape[1]
    n_seg = rows // seg

    if carried:
        @pl.when(pl.program_id(1) == 0)
        def _():
            carry[...] = jnp.zeros(carry.shape, carry.dtype)

    mixed = jnp.concatenate([m_ref[0, s] for s in range(m_ref.shape[1])], axis=-1)
    x1 = x_ref[0] + jnp.dot(mixed, wo_ref[...], preferred_element_type=_F32)
    h = _rmsnorm_rows(x1, n2_ref[...]).astype(_BF16)
    r = jnp.bitwise_and(lax.broadcasted_iota(jnp.int32, (rows, 1), 0), seg - 1)

    def conv(col):
        u = jnp.dot(h, wu_ref[:, col:col + FF_CHUNK], preferred_element_type=_F32)
        cs = slice(col, col + FF_CHUNK)
        if carried:
            p0, p1 = carry[SUBLANES - 2:SUBLANES - 1, cs], carry[SUBLANES - 1:SUBLANES, cs]
            carry[:, cs] = u[rows - SUBLANES:, :]
        else:
            per_seg = lambda j: jnp.concatenate(
                [jnp.broadcast_to(prev_ref[g, j:j + 1, cs], (seg, FF_CHUNK)) for g in range(n_seg)], axis=0)
            p0, p1 = per_seg(0), per_seg(1)
        for g in range(n_seg):
            st_ref[g, :, cs] = u[(g + 1) * seg - SUBLANES:(g + 1) * seg, :]
        u1 = jnp.where(r == 0, p1, pltpu.roll(u, 1, 0))
        u2 = jnp.where(r == 0, p0, jnp.where(r == 1, p1, pltpu.roll(u, 2, 0)))
        return cb_ref[:, cs] + cw_ref[0:1, cs] * u2 + cw_ref[1:2, cs] * u1 + cw_ref[2:3, cs] * u

    for c in range(0, d_ff, FF_CHUNK):
        a = conv(c)
        g = conv(d_ff + c)
        act[:, c:c + FF_CHUNK] = (_gelu(a) * g).astype(act.dtype)

    y_ref[0] = x1 + jnp.dot(act[...], wd_ref[...], preferred_element_type=_F32)


def _ffn(x, mixed, w_out, n2, w_up, conv_w, conv_b, w_down, prev=None, *, layer, tile, seg):
    bsz, seq, d_model = x.shape
    d_ff = w_down.shape[1]
    n_tiles = seq // tile
    carried = prev is None
    n_seg = 1 if carried else tile // seg
    assert seg & (seg - 1) == 0 and (seg == seq if carried else tile % seg == 0)

    def row(b, i):
        return (b, i, 0)

    def slab_row(b, i):
        return (b, 0, i, 0)

    def segs(b, i):
        return (b if carried else b * n_tiles + i, 0, 0)

    def const2(b, i):
        return (0, 0)

    operands = [x, mixed, w_out, n2, w_up, conv_w, conv_b, w_down]
    in_specs = [pl.BlockSpec((1, tile, d_model), row),
                pl.BlockSpec((1, mixed.shape[1], tile, LANES), slab_row),
                _layer_resident(w_out, layer), _resident((1, d_model), const2),
                _layer_resident(w_up, layer), _resident(conv_w.shape, const2),
                _resident((1, 2 * d_ff), const2), _layer_resident(w_down, layer)]
    scratch = [pltpu.VMEM((tile, d_ff), _BF16)]
    if carried:
        scratch.insert(0, pltpu.VMEM((SUBLANES, 2 * d_ff), _F32))
    else:
        operands.append(prev)
        in_specs.append(pl.BlockSpec((n_seg,) + prev.shape[1:], segs))
    n_state = bsz * (1 if carried else n_tiles * n_seg)
    out_specs = (pl.BlockSpec((1, tile, d_model), row), pl.BlockSpec((n_seg, SUBLANES, 2 * d_ff), segs))
    out_shape = (jax.ShapeDtypeStruct((bsz, seq, d_model), _F32),
                 jax.ShapeDtypeStruct((n_state, SUBLANES, 2 * d_ff), _F32))
    kern = functools.partial(_ffn_kernel, d_ff=d_ff, seg=tile if carried else seg, carried=carried)
    return pl.pallas_call(
        kern, grid=(bsz, n_tiles), in_specs=in_specs, out_specs=out_specs, out_shape=out_shape,
        scratch_shapes=scratch, compiler_params=_params(2), name="ffn")(*operands)


def _rel_rows(table):
    n_low = B_REACH - REL_CLIP + ATTN_SUB - 1
    n_high = REL_ROW - n_low - table.shape[0]
    rows = jnp.concatenate([jnp.broadcast_to(table[-1:], (n_low, table.shape[1])),
                            table[::-1],
                            jnp.broadcast_to(table[:1], (n_high, table.shape[1]))], axis=0)
    return rows.T


def _tile2(g, scale=1.0):
    return (jnp.tile(g, 2) * scale).reshape(1, LANES)


def _slab_major(t, twice=False):
    b, r, c = t.shape
    if twice:
        t = jnp.repeat(t.reshape(b, r, c // HEAD_DIM, 1, HEAD_DIM), 2, axis=3).reshape(b, r, 2 * c)
    return jnp.swapaxes(t.reshape(b, r, -1, LANES), 1, 2)


def _split_batches(t, nb):
    _, ns, rows, _ = t.shape
    return jnp.swapaxes(t.reshape(ns, nb, rows // nb, LANES), 0, 1)


def kernel(x_prompt, x_sample, cache_a_k, cache_a_v, cache_b_k, cache_b_v, cache_ffn_conv, norm1_g, w_in, b_gate, qn_a_g, kn_a_g, qn_b_g, kn_b_g, sinks_a, rel_bias_b, w_out, norm2_g, w_up, conv_w, conv_b, w_down):
    depth = w_in.shape[0]
    bsz, seq, d_model = x_prompt.shape
    dbs, dseq, _ = x_sample.shape
    n_heads = d_model // HEAD_DIM
    assert cache_a_k.shape[2] == A_REACH and cache_b_k.shape[2] == B_REACH
    assert seq % PROJ_TILE == 0 and PROJ_TILE == B_REACH and dseq <= CHUNK

    slopes = 2.0 ** (-8.0 * jnp.arange(1, n_heads + 1, dtype=_F32) / n_heads)
    xp, xs = x_prompt, x_sample.reshape(1, dbs * dseq, d_model)
    heads = lambda t, rows: t.reshape(-1, rows, t.shape[-1] // HEAD_DIM, HEAD_DIM)
    w_in, w_out, w_up, w_down = (_to_bf16(w) for w in (w_in, w_out, w_up, w_down))
    pk, sk = [], []
    for l in range(depth):
        n1, n2 = norm1_g[l].reshape(1, -1), norm2_g[l].reshape(1, -1)
        bg, cb = b_gate[l].reshape(1, -1), conv_b[l].reshape(1, -1)
        gains = (_tile2(qn_a_g[l], Q_SCALE), _tile2(kn_a_g[l]), _tile2(qn_b_g[l], Q_SCALE), _tile2(kn_b_g[l]))
        rel = _rel_rows(rel_bias_b[l])

        qa, qb, g, ka, va, kb, vb, kat, vat, kbt, vbt = _proj(
            xp, n1, w_in, bg, *gains, layer=l, tile=PROJ_TILE, n_pad=1)
        mixed = _attention(slopes, sinks_a[l], rel, qa, qb, g, ka, va, kb, vb, tq=ATTN_TILE, windowed=True)
        xp, st = _ffn(xp, mixed, w_out, n2, w_up, conv_w[l], cb, w_down, layer=l, tile=PROJ_TILE, seg=seq)
        pk.append((heads(kat[:, -A_REACH:], A_REACH), heads(vat[:, -A_REACH:], A_REACH),
                   heads(kbt, B_REACH), heads(vbt, B_REACH), st[:, -(CONV_W - 1):]))

        qa, qb, g, ka, va, kb, vb, kat, vat, kbt, vbt = _proj(
            xs, n1, w_in, bg, *gains, layer=l, tile=dbs * dseq, n_pad=0)

        def win(cache, new, twice=False, feature_major=False):
            rows = _window(cache.shape[1], dseq)
            old = _slab_major(cache.reshape(dbs, cache.shape[1], -1).astype(_BF16), twice)
            if feature_major:
                new = jnp.swapaxes(new, 2, 3)
            w = jnp.concatenate([old, _split_batches(new, dbs)], axis=2)
            w = jnp.pad(w, ((0, 0), (0, 0), (0, rows - w.shape[2]), (0, 0)))
            return jnp.swapaxes(w, 2, 3) if feature_major else w

        mixed = _attention(slopes, sinks_a[l], rel, qa, qb, g,
                           win(cache_a_k[l], ka, True, True), win(cache_a_v[l], va, True),
                           win(cache_b_k[l], kb, False, True), win(cache_b_v[l], vb), tq=dseq, windowed=False)
        xs, st = _ffn(xs, mixed, w_out, n2, w_up, conv_w[l], cb, w_down, cache_ffn_conv[l],
                      layer=l, tile=dbs * dseq, seg=dseq)
        sk.append((heads(kat, dseq), heads(vat, dseq), heads(kbt, dseq), heads(vbt, dseq),
                   st[:, -(CONV_W - 1):]))

    stk = lambda states, i: jnp.stack([s[i] for s in states])
    return (xp, xs.reshape(dbs, dseq, d_model),
            stk(pk, 0), stk(pk, 1), stk(pk, 2), stk(pk, 3), stk(pk, 4),
            stk(sk, 0), stk(sk, 1), stk(sk, 2), stk(sk, 3), stk(sk, 4))
```

```python
import functools

import jax
import jax.numpy as jnp
from jax import lax
from jax.experimental import pallas as pl
from jax.experimental.pallas import tpu as pltpu

HEAD_DIM = 64
CHUNK = 64
A_GROUP = 4
A_PREV = 2
B_PREV = 8
A_REACH = A_PREV * CHUNK
B_REACH = B_PREV * CHUNK
REL_CLIP = 128
CONV_W = 3
EPS = 1e-6
NEG_INF = -1e30
LOG2E = 1.4426950408889634
Q_SCALE = HEAD_DIM ** -0.5 * LOG2E

LANES = 128
SUBLANES = 8
MXU_WIDTH = 256
VMEM_LIMIT_BYTES = 56 * 1024 * 1024

PROJ_TILE = 512
ATTN_TILE = 256
ATTN_SUB = 128
FF_CHUNK = 256
REL_ROW = 1024

_BF16 = jnp.bfloat16
_F32 = jnp.float32


def _resident(shape, index_map):
    return pl.BlockSpec(shape, index_map, pipeline_mode=pl.Buffered(1))


def _layer_resident(w, layer):
    return pl.BlockSpec((None,) + w.shape[1:], lambda *_: (layer, 0, 0), pipeline_mode=pl.Buffered(1))


def _params(n_axes, flags=None):
    return pltpu.CompilerParams(dimension_semantics=("arbitrary",) * n_axes,
                                vmem_limit_bytes=VMEM_LIMIT_BYTES, flags=flags)


def _rmsnorm_rows(x, g):
    return x * lax.rsqrt(jnp.mean(x * x, axis=-1, keepdims=True) + EPS) * g


def _lane_is_low():
    return lax.broadcasted_iota(jnp.int32, (1, LANES), 1) < HEAD_DIM


def _headnorm_slab(z, g2):
    low = _lane_is_low()
    sq = z * z
    s_lo = jnp.sum(jnp.where(low, sq, 0.0), axis=-1, keepdims=True)
    s_hi = jnp.sum(jnp.where(low, 0.0, sq), axis=-1, keepdims=True)
    ms = jnp.where(low, s_lo, s_hi) * (1.0 / HEAD_DIM)
    return z * lax.rsqrt(ms + EPS) * g2


CAST_BLOCK_BYTES = 6 * 1024 * 1024


def _cast_kernel(x_ref, o_ref):
    o_ref[...] = x_ref[...].astype(o_ref.dtype)


def _to_bf16(w):
    depth, rows, cols = w.shape
    packing = 2 * SUBLANES
    fits = [r for r in range(packing, rows + 1, packing) if rows % r == 0 and r * cols * 4 <= CAST_BLOCK_BYTES]
    rb = max(fits)
    spec = pl.BlockSpec((1, rb, cols), lambda l, i: (l, i, 0))
    return pl.pallas_call(_cast_kernel, grid=(depth, rows // rb), in_specs=[spec], out_specs=spec,
                          out_shape=jax.ShapeDtypeStruct(w.shape, _BF16), compiler_params=_params(2),
                          name="cast")(w)


def _proj_kernel(x_ref, n1_ref, w_ref, bg_ref, gqa_ref, gka_ref, gqb_ref, gkb_ref,
                 qa_ref, qb_ref, g_ref, ka_ref, va_ref, kb_ref, vb_ref,
                 kat_ref, vat_ref, kbt_ref, vbt_ref, *, d_model, a_kv):
    o_ka = d_model
    o_va = o_ka + a_kv
    o_qb = o_va + a_kv
    o_kb = o_qb + d_model
    o_vb = o_kb + d_model
    o_g = o_vb + d_model

    h = _rmsnorm_rows(x_ref[0], n1_ref[...]).astype(_BF16)
    low = _lane_is_low()

    def slabs(base, width):
        for c in range(0, width, MXU_WIDTH):
            z = jnp.dot(h, w_ref[:, base + c:base + c + MXU_WIDTH], preferred_element_type=_F32)
            for t in range(MXU_WIDTH // LANES):
                yield c // LANES + t, z[:, t * LANES:(t + 1) * LANES]

    def emit(out_ref, base, width, gain_ref=None, tail_ref=None, twice=False, feature_major=False):
        def put(idx, y):
            out_ref[0, idx] = (y.T if feature_major else y).astype(out_ref.dtype)

        for s, y in slabs(base, width):
            if gain_ref is not None:
                y = _headnorm_slab(y, gain_ref[...])
            if tail_ref is not None:
                tail_ref[0, :, s * LANES:(s + 1) * LANES] = y
            if twice:
                y_sw = pltpu.roll(y, HEAD_DIM, 1)
                put(2 * s, jnp.where(low, y, y_sw))
                put(2 * s + 1, jnp.where(low, y_sw, y))
            else:
                put(s, y)

    emit(qa_ref, 0, d_model, gqa_ref)
    emit(ka_ref, o_ka, a_kv, gka_ref, kat_ref, twice=True, feature_major=True)
    emit(va_ref, o_va, a_kv, None, vat_ref, twice=True)
    emit(qb_ref, o_qb, d_model, gqb_ref)
    emit(kb_ref, o_kb, d_model, gkb_ref, kbt_ref, feature_major=True)
    emit(vb_ref, o_vb, d_model, None, vbt_ref)
    for s, y in slabs(o_g, 2 * d_model):
        c = s * LANES
        g_ref[0, s] = jax.nn.sigmoid(y + bg_ref[:, c:c + LANES]).astype(g_ref.dtype)


def _proj(x, n1, w_in, b_gate, gqa, gka, gqb, gkb, *, layer, tile):
    bsz, seq, d_model = x.shape
    a_kv = d_model // A_GROUP
    ns_d, ns_kv = d_model // LANES, a_kv // LANES

    def row(b, j):
        return (b, j, 0)

    def slab_row(b, j):
        return (b, 0, j, 0)

    def slab_col(b, j):
        return (b, 0, 0, j)

    def const2(b, j):
        return (0, 0)

    def tail(b, j):
        return (b, 0, 0)

    bf = lambda ns: jax.ShapeDtypeStruct((bsz, ns, seq, LANES), _BF16)
    bf_t = lambda ns: jax.ShapeDtypeStruct((bsz, ns, LANES, seq), _BF16)
    f32 = lambda ns: jax.ShapeDtypeStruct((bsz, tile, ns * LANES), _F32)
    out_shape = (bf(ns_d), bf(ns_d), bf(2 * ns_d), bf_t(2 * ns_kv), bf(2 * ns_kv), bf_t(ns_d), bf(ns_d),
                 f32(ns_kv), f32(ns_kv), f32(ns_d), f32(ns_d))
    blk = lambda ns: pl.BlockSpec((1, ns, tile, LANES), slab_row)
    blk_t = lambda ns: pl.BlockSpec((1, ns, LANES, tile), slab_col)
    blk_tail = lambda ns: pl.BlockSpec((1, tile, ns * LANES), tail)
    out_specs = (blk(ns_d), blk(ns_d), blk(2 * ns_d), blk_t(2 * ns_kv), blk(2 * ns_kv), blk_t(ns_d), blk(ns_d),
                 blk_tail(ns_kv), blk_tail(ns_kv), blk_tail(ns_d), blk_tail(ns_d))
    in_specs = [pl.BlockSpec((1, tile, d_model), row),
                _resident((1, d_model), const2),
                _layer_resident(w_in, layer),
                _resident((1, 2 * d_model), const2),
                _resident((1, LANES), const2), _resident((1, LANES), const2),
                _resident((1, LANES), const2), _resident((1, LANES), const2)]
    kern = functools.partial(_proj_kernel, d_model=d_model, a_kv=a_kv)
    return pl.pallas_call(kern, grid=(bsz, seq // tile), in_specs=in_specs, out_specs=out_specs,
                          out_shape=out_shape, compiler_params=_params(2), name="proj")(
                              x, n1, w_in, b_gate, gqa, gka, gqb, gkb)


def _band_mask(tq, w, n_prev):
    shift = CHUNK.bit_length() - 1
    qc = lax.shift_right_logical(lax.broadcasted_iota(jnp.int32, (tq, w), 0), shift)
    kc = lax.shift_right_logical(lax.broadcasted_iota(jnp.int32, (tq, w), 1), shift)
    d = kc - qc
    return (d >= 0) & (d <= n_prev)


def _window(reach, tq):
    return -(-(reach + tq) // LANES) * LANES


def _attn_kernel(*refs, tq, ts, wa, wb, n_blocks, n_slabs):
    na, nb = n_blocks
    slopes_ref, sinks_ref, rel_ref, qa_ref, qb_ref, g_ref = refs[:6]
    kv = list(refs[6:6 + 2 * (na + nb)])
    ka_refs, va_refs, kb_refs, vb_refs = kv[:na], kv[na:2 * na], kv[2 * na:2 * na + nb], kv[2 * na + nb:]
    o_ref, bias_a, bias_b = refs[6 + len(kv):9 + len(kv)]
    bufs = refs[9 + len(kv):]
    sa_ref, sb_ref, pa_ref, pb_ref, da_ref = (bufs[2 * n:2 * n + 2] for n in range(5))
    windowed = na > 1
    b = pl.program_id(0)
    i = pl.program_id(1)

    @pl.when((b == 0) & (i == 0))
    def _():
        qpos = lax.broadcasted_iota(jnp.int32, (ts, wa), 0)
        kpos = lax.broadcasted_iota(jnp.int32, (ts, wa), 1) - A_REACH
        dist = jnp.abs(qpos - kpos).astype(_F32)
        band_a = _band_mask(ts, wa, A_PREV)
        band_b = _band_mask(ts, wb, B_PREV)
        for h in range(2 * n_slabs):
            rows_h = slice((h % 2) * ts, (h % 2 + 1) * ts)
            bias_a[h // 2, rows_h, :] = jnp.where(band_a, (-LOG2E * slopes_ref[h]) * dist, NEG_INF)
            rows = jnp.broadcast_to(rel_ref[h:h + 1, :], (ts, REL_ROW))
            rows = pltpu.roll(rows, REL_ROW - ATTN_SUB + 1, 1, stride=1, stride_axis=0)
            bias_b[h // 2, rows_h, :] = jnp.where(band_b, LOG2E * rows[:, :wb], NEG_INF)

    low = _lane_is_low()
    row_is_lo = lax.broadcasted_iota(jnp.int32, (2 * ts, 1), 0) < ts
    col_a = lax.broadcasted_iota(jnp.int32, (1, wa), 1)
    col_b = lax.broadcasted_iota(jnp.int32, (1, wb), 1)
    ones = jnp.ones((max(wa, wb), LANES), _BF16)

    def window(blocks, slab, u, reach, w, feature_major):
        if feature_major:
            cut = lambda r, lo, hi: r[0, slab, :, lo:hi]
        else:
            cut = lambda r, lo, hi: r[0, slab, lo:hi, :]
        if len(blocks) == 1:
            return cut(blocks[0], 0, w)
        lo, hi = u * ts - reach, (u + 1) * ts
        pieces = []
        for j, r in enumerate(blocks):
            base = (j - len(blocks) + 1) * tq
            if max(lo, base) < min(hi, base + tq):
                pieces.append(cut(r, max(lo, base) - base, min(hi, base + tq) - base))
        return pieces[0] if len(pieces) == 1 else jnp.concatenate(pieces, axis=1 if feature_major else 0)

    def stacked(qs):
        zero = jnp.zeros_like(qs)
        return jnp.concatenate([jnp.where(low, qs, zero), jnp.where(low, zero, qs)], axis=0)

    def softmax_weights(s, sink_col):
        m = jnp.max(s, axis=-1, keepdims=True)
        if sink_col is not None:
            m = jnp.maximum(m, sink_col)
        p = jnp.exp2(s - m).astype(_BF16)
        return p, None if sink_col is None else jnp.broadcast_to(jnp.exp2(sink_col - m), (2 * ts, LANES))

    def weighted(p, v, extra_den):
        o2 = jnp.dot(p, jnp.concatenate([v, ones[:v.shape[0]]], axis=1), preferred_element_type=_F32)
        den = o2[:, LANES:]
        if extra_den is not None:
            den = den + extra_den
        o = o2[:, :LANES] / den
        return jnp.where(low, o[:ts], o[ts:])

    def run(masked):
        valid = []
        for u in range(tq // ts):
            if not windowed:
                valid.append((jnp.where(col_a < A_REACH + tq, 0.0, NEG_INF),
                              jnp.where(col_b < B_REACH + tq, 0.0, NEG_INF)))
            elif masked:
                start = i * tq + u * ts
                valid.append((jnp.where(col_a >= A_REACH - start, 0.0, NEG_INF),
                              jnp.where(col_b >= B_REACH - start, 0.0, NEG_INF)))
            else:
                valid.append((None, None))

        def scores(k, u, par):
            rows_u = slice(u * ts, (u + 1) * ts)
            sa_ref[par][...] = jnp.dot(stacked(qa_ref[0, k, rows_u]), window(ka_refs, k // 2, u, A_REACH, wa, True),
                                       preferred_element_type=_F32)
            sb_ref[par][...] = jnp.dot(stacked(qb_ref[0, k, rows_u]), window(kb_refs, k, u, B_REACH, wb, True),
                                       preferred_element_type=_F32)

        def softmax(k, u, par):
            valid_a, valid_b = valid[u]
            sink_col = LOG2E * jnp.where(row_is_lo, sinks_ref[2 * k], sinks_ref[2 * k + 1])
            s = sa_ref[par][...] + bias_a[k]
            pa_ref[par][...], da_ref[par][...] = softmax_weights(s if valid_a is None else s + valid_a, sink_col)
            s = sb_ref[par][...] + bias_b[k]
            pb_ref[par][...], _ = softmax_weights(s if valid_b is None else s + valid_b, None)

        def values(k, u, par):
            rows_u = slice(u * ts, (u + 1) * ts)
            oa = weighted(pa_ref[par][...], window(va_refs, k // 2, u, A_REACH, wa, False), da_ref[par][...])
            ob = weighted(pb_ref[par][...], window(vb_refs, k, u, B_REACH, wb, False), None)
            mixed = g_ref[0, k, rows_u].astype(_F32) * oa + g_ref[0, n_slabs + k, rows_u].astype(_F32) * ob
            o_ref[0, k, rows_u] = mixed.astype(o_ref.dtype)

        if tq // ts == 2:
            scores(0, 0, 0)
            softmax(0, 0, 0)
            scores(0, 1, 1)

            def body(k, carry):
                values(k - 1, 0, 0)
                softmax(k - 1, 1, 1)
                scores(k, 0, 0)
                values(k - 1, 1, 1)
                softmax(k, 0, 0)
                scores(k, 1, 1)
                return carry

            lax.fori_loop(1, n_slabs, body, 0)
            values(n_slabs - 1, 0, 0)
            softmax(n_slabs - 1, 1, 1)
            values(n_slabs - 1, 1, 1)
        else:
            scores(0, 0, 0)
            softmax(0, 0, 0)
            scores(1, 0, 1)

            def body(j, carry):
                k = 2 * j + 2
                values(k - 2, 0, 0)
                softmax(k - 1, 0, 1)
                scores(k, 0, 0)
                values(k - 1, 0, 1)
                softmax(k, 0, 0)
                scores(k + 1, 0, 1)
                return carry

            lax.fori_loop(0, (n_slabs - 2) // 2, body, 0)
            values(n_slabs - 2, 0, 0)
            softmax(n_slabs - 1, 0, 1)
            values(n_slabs - 1, 0, 1)

    if windowed:
        n_early = B_REACH // tq
        pl.when(i < n_early)(lambda: run(True))
        pl.when(i >= n_early)(lambda: run(False))
    else:
        run(True)


def _attention(slopes, sinks, rel_rows, qa, qb, gates, ka, va, kb, vb, *, tq, windowed):
    _, n_slabs, seq, _ = qa.shape
    bsz = ka.shape[0]
    ts = min(tq, ATTN_SUB)
    wa = _window(A_REACH, ts)
    wb = _window(B_REACH, ts)
    n_tiles = seq // tq if windowed else 1
    n_blocks = (1 + -(-A_REACH // tq), 1 + -(-B_REACH // tq)) if windowed else (1, 1)

    def row(b, i):
        return (b, 0, i, 0) if windowed else (0, 0, b, 0)

    def const2(b, i):
        return (0, 0)

    def kv_specs(t, n, feature_major):
        if not windowed:
            return [pl.BlockSpec((1,) + t.shape[1:], lambda b, i: (b, 0, 0, 0))]
        if feature_major:
            return [pl.BlockSpec((1, t.shape[1], LANES, tq), lambda b, i, d=d: (b, 0, 0, jnp.maximum(i - d, 0)))
                    for d in range(n - 1, -1, -1)]
        return [pl.BlockSpec((1, t.shape[1], tq, LANES), lambda b, i, d=d: (b, 0, jnp.maximum(i - d, 0), 0))
                for d in range(n - 1, -1, -1)]

    smem = pl.BlockSpec(memory_space=pltpu.SMEM)
    rows = lambda ns: pl.BlockSpec((1, ns, tq, LANES), row)
    na, nb = n_blocks
    in_specs = ([smem, smem, _resident(rel_rows.shape, const2), rows(n_slabs), rows(n_slabs), rows(2 * n_slabs)]
                + kv_specs(ka, na, True) + kv_specs(va, na, False) + kv_specs(kb, nb, True) + kv_specs(vb, nb, False))
    operands = [slopes, sinks, rel_rows, qa, qb, gates] + [ka] * na + [va] * na + [kb] * nb + [vb] * nb
    kern = functools.partial(_attn_kernel, tq=tq, ts=ts, wa=wa, wb=wb, n_blocks=n_blocks, n_slabs=n_slabs)
    return pl.pallas_call(
        kern, grid=(bsz, n_tiles), in_specs=in_specs, out_specs=rows(n_slabs),
        out_shape=jax.ShapeDtypeStruct(qa.shape, _BF16),
        scratch_shapes=[pltpu.VMEM((n_slabs, 2 * ts, wa), _F32), pltpu.VMEM((n_slabs, 2 * ts, wb), _F32)]
        + [pltpu.VMEM((2 * ts, w), dt)
           for w, dt in ((wa, _F32), (wb, _F32), (wa, _BF16), (wb, _BF16), (LANES, _F32)) for _ in range(2)],
        compiler_params=_params(2), name="attn")(*operands)


def _gelu(x):
    return 0.5 * x * (1.0 + lax.erf(x * (2.0 ** -0.5)))


def _ffn_kernel(*refs, d_ff, seg, carried):
    if carried:
        x_ref, m_ref, wo_ref, n2_ref, wu_ref, cw_ref, cb_ref, wd_ref, y_ref, st_ref, carry, act = refs
    else:
        x_ref, m_ref, wo_ref, n2_ref, wu_ref, cw_ref, cb_ref, wd_ref, prev_ref, y_ref, st_ref, act = refs
    rows = x_ref.shape[1]
    n_seg = rows // seg

    if carried:
        @pl.when(pl.program_id(1) == 0)
        def _():
            carry[...] = jnp.zeros(carry.shape, carry.dtype)

    mixed = jnp.concatenate([m_ref[0, s] for s in range(m_ref.shape[1])], axis=-1)
    x1 = x_ref[0] + jnp.dot(mixed, wo_ref[...], preferred_element_type=_F32)
    h = _rmsnorm_rows(x1, n2_ref[...]).astype(_BF16)
    r = jnp.bitwise_and(lax.broadcasted_iota(jnp.int32, (rows, 1), 0), seg - 1)

    def conv(col):
        u = jnp.dot(h, wu_ref[:, col:col + FF_CHUNK], preferred_element_type=_F32)
        cs = slice(col, col + FF_CHUNK)
        if carried:
            p0, p1 = carry[SUBLANES - 2:SUBLANES - 1, cs], carry[SUBLANES - 1:SUBLANES, cs]
            carry[:, cs] = u[rows - SUBLANES:, :]
        else:
            per_seg = lambda j: jnp.concatenate(
                [jnp.broadcast_to(prev_ref[g, j:j + 1, cs], (seg, FF_CHUNK)) for g in range(n_seg)], axis=0)
            p0, p1 = per_seg(0), per_seg(1)
        for g in range(n_seg):
            st_ref[g, :, cs] = u[(g + 1) * seg - SUBLANES:(g + 1) * seg, :]
        u1 = jnp.where(r == 0, p1, pltpu.roll(u, 1, 0))
        u2 = jnp.where(r == 0, p0, jnp.where(r == 1, p1, pltpu.roll(u, 2, 0)))
        return cb_ref[:, cs] + cw_ref[0:1, cs] * u2 + cw_ref[1:2, cs] * u1 + cw_ref[2:3, cs] * u

    for c in range(0, d_ff, FF_CHUNK):
        a = conv(c)
        g = conv(d_ff + c)
        act[:, c:c + FF_CHUNK] = (_gelu(a) * g).astype(act.dtype)

    y_ref[0] = x1 + jnp.dot(act[...], wd_ref[...], preferred_element_type=_F32)


def _ffn(x, mixed, w_out, n2, w_up, conv_w, conv_b, w_down, prev=None, *, layer, tile, seg):
    bsz, seq, d_model = x.shape
    d_ff = w_down.shape[1]
    n_tiles = seq // tile
    carried = prev is None
    n_seg = 1 if carried else tile // seg
    assert seg & (seg - 1) == 0 and (seg == seq if carried else tile % seg == 0)

    def row(b, i):
        return (b, i, 0)

    def slab_row(b, i):
        return (b, 0, i, 0)

    def segs(b, i):
        return (b if carried else b * n_tiles + i, 0, 0)

    def const2(b, i):
        return (0, 0)

    operands = [x, mixed, w_out, n2, w_up, conv_w, conv_b, w_down]
    in_specs = [pl.BlockSpec((1, tile, d_model), row),
                pl.BlockSpec((1, mixed.shape[1], tile, LANES), slab_row),
                _layer_resident(w_out, layer), _resident((1, d_model), const2),
                _layer_resident(w_up, layer), _resident(conv_w.shape, const2),
                _resident((1, 2 * d_ff), const2), _layer_resident(w_down, layer)]
    scratch = [pltpu.VMEM((tile, d_ff), _BF16)]
    if carried:
        scratch.insert(0, pltpu.VMEM((SUBLANES, 2 * d_ff), _F32))
    else:
        operands.append(prev)
        in_specs.append(pl.BlockSpec((n_seg,) + prev.shape[1:], segs))
    n_state = bsz * (1 if carried else n_tiles * n_seg)
    out_specs = (pl.BlockSpec((1, tile, d_model), row), pl.BlockSpec((n_seg, SUBLANES, 2 * d_ff), segs))
    out_shape = (jax.ShapeDtypeStruct((bsz, seq, d_model), _F32),
                 jax.ShapeDtypeStruct((n_state, SUBLANES, 2 * d_ff), _F32))
    kern = functools.partial(_ffn_kernel, d_ff=d_ff, seg=tile if carried else seg, carried=carried)
    return pl.pallas_call(
        kern, grid=(bsz, n_tiles), in_specs=in_specs, out_specs=out_specs, out_shape=out_shape,
        scratch_shapes=scratch, compiler_params=_params(2), name="ffn")(*operands)


def _rel_rows(table):
    n_low = B_REACH - REL_CLIP + ATTN_SUB - 1
    n_high = REL_ROW - n_low - table.shape[0]
    rows = jnp.concatenate([jnp.broadcast_to(table[-1:], (n_low, table.shape[1])),
                            table[::-1],
                            jnp.broadcast_to(table[:1], (n_high, table.shape[1]))], axis=0)
    return rows.T


def _tile2(g, scale=1.0):
    return (jnp.tile(g, 2) * scale).reshape(1, LANES)


def _slab_major(t, twice=False):
    b, r, c = t.shape
    if twice:
        t = jnp.repeat(t.reshape(b, r, c // HEAD_DIM, 1, HEAD_DIM), 2, axis=3).reshape(b, r, 2 * c)
    return jnp.swapaxes(t.reshape(b, r, -1, LANES), 1, 2)


def _split_batches(t, nb):
    _, ns, rows, _ = t.shape
    return jnp.swapaxes(t.reshape(ns, nb, rows // nb, LANES), 0, 1)


def kernel(x_prompt, x_sample, cache_a_k, cache_a_v, cache_b_k, cache_b_v, cache_ffn_conv, norm1_g, w_in, b_gate, qn_a_g, kn_a_g, qn_b_g, kn_b_g, sinks_a, rel_bias_b, w_out, norm2_g, w_up, conv_w, conv_b, w_down):
    depth = w_in.shape[0]
    bsz, seq, d_model = x_prompt.shape
    dbs, dseq, _ = x_sample.shape
    n_heads = d_model // HEAD_DIM
    assert cache_a_k.shape[2] == A_REACH and cache_b_k.shape[2] == B_REACH
    assert seq % PROJ_TILE == 0 and PROJ_TILE >= B_REACH and dseq <= CHUNK

    slopes = 2.0 ** (-8.0 * jnp.arange(1, n_heads + 1, dtype=_F32) / n_heads)
    xp, xs = x_prompt, x_sample.reshape(1, dbs * dseq, d_model)
    heads = lambda t, rows: t.reshape(-1, rows, t.shape[-1] // HEAD_DIM, HEAD_DIM)
    w_in, w_out, w_up, w_down = (_to_bf16(w) for w in (w_in, w_out, w_up, w_down))
    pk, sk = [], []
    for l in range(depth):
        n1, n2 = norm1_g[l].reshape(1, -1), norm2_g[l].reshape(1, -1)
        bg, cb = b_gate[l].reshape(1, -1), conv_b[l].reshape(1, -1)
        gains = (_tile2(qn_a_g[l], Q_SCALE), _tile2(kn_a_g[l]), _tile2(qn_b_g[l], Q_SCALE), _tile2(kn_b_g[l]))
        rel = _rel_rows(rel_bias_b[l])

        qa, qb, g, ka, va, kb, vb, kat, vat, kbt, vbt = _proj(
            xp, n1, w_in, bg, *gains, layer=l, tile=PROJ_TILE)
        mixed = _attention(slopes, sinks_a[l], rel, qa, qb, g, ka, va, kb, vb, tq=ATTN_TILE, windowed=True)
        xp, st = _ffn(xp, mixed, w_out, n2, w_up, conv_w[l], cb, w_down, layer=l, tile=PROJ_TILE, seg=seq)
        pk.append((heads(kat[:, -A_REACH:], A_REACH), heads(vat[:, -A_REACH:], A_REACH),
                   heads(kbt, B_REACH), heads(vbt, B_REACH), st[:, -(CONV_W - 1):]))

        qa, qb, g, ka, va, kb, vb, kat, vat, kbt, vbt = _proj(
            xs, n1, w_in, bg, *gains, layer=l, tile=dbs * dseq)

        def win(cache, new, twice=False, feature_major=False):
            rows = _window(cache.shape[1], dseq)
            old = _slab_major(cache.reshape(dbs, cache.shape[1], -1).astype(_BF16), twice)
            if feature_major:
                new = jnp.swapaxes(new, 2, 3)
            w = jnp.concatenate([old, _split_batches(new, dbs)], axis=2)
            w = jnp.pad(w, ((0, 0), (0, 0), (0, rows - w.shape[2]), (0, 0)))
            return jnp.swapaxes(w, 2, 3) if feature_major else w

        mixed = _attention(slopes, sinks_a[l], rel, qa, qb, g,
                           win(cache_a_k[l], ka, True, True), win(cache_a_v[l], va, True),
                           win(cache_b_k[l], kb, False, True), win(cache_b_v[l], vb), tq=dseq, windowed=False)
        xs, st = _ffn(xs, mixed, w_out, n2, w_up, conv_w[l], cb, w_down, cache_ffn_conv[l],
                      layer=l, tile=dbs * dseq, seg=dseq)
        sk.append((heads(kat, dseq), heads(vat, dseq), heads(kbt, dseq), heads(vbt, dseq),
                   st[:, -(CONV_W - 1):]))

    stk = lambda states, i: jnp.stack([s[i] for s in states])
    return (xp, xs.reshape(dbs, dseq, d_model),
            stk(pk, 0), stk(pk, 1), stk(pk, 2), stk(pk, 3), stk(pk, 4),
            stk(sk, 0), stk(sk, 1), stk(sk, 2), stk(sk, 3), stk(sk, 4))
```

```python
import functools

import jax
import jax.numpy as jnp
from jax import lax
from jax.experimental import pallas as pl
from jax.experimental.pallas import tpu as pltpu

HEAD_DIM = 64
CHUNK = 64
A_GROUP = 4
A_PREV = 2
B_PREV = 8
A_REACH = A_PREV * CHUNK
B_REACH = B_PREV * CHUNK
REL_CLIP = 128
CONV_W = 3
EPS = 1e-6
NEG_INF = -1e30
LOG2E = 1.4426950408889634
Q_SCALE = HEAD_DIM ** -0.5 * LOG2E

LANES = 128
SUBLANES = 8
MXU_WIDTH = 256
VMEM_LIMIT_BYTES = 56 * 1024 * 1024

PROJ_TILE = 512
ATTN_TILE = 256
ATTN_SUB = 128
FF_CHUNK = 256
REL_ROW = 1024

_BF16 = jnp.bfloat16
_F32 = jnp.float32


def _resident(shape, index_map):
    return pl.BlockSpec(shape, index_map, pipeline_mode=pl.Buffered(1))


def _layer_resident(w, layer):
    return pl.BlockSpec((None,) + w.shape[1:], lambda *_: (layer, 0, 0), pipeline_mode=pl.Buffered(1))


def _params(n_axes, flags=None):
    return pltpu.CompilerParams(dimension_semantics=("arbitrary",) * n_axes,
                                vmem_limit_bytes=VMEM_LIMIT_BYTES, flags=flags)


def _rmsnorm_rows(x, g):
    return x * lax.rsqrt(jnp.mean(x * x, axis=-1, keepdims=True) + EPS) * g


def _lane_is_low():
    return lax.broadcasted_iota(jnp.int32, (1, LANES), 1) < HEAD_DIM


def _headnorm_slab(z, g2):
    low = _lane_is_low()
    sq = z * z
    s_lo = jnp.sum(jnp.where(low, sq, 0.0), axis=-1, keepdims=True)
    s_hi = jnp.sum(jnp.where(low, 0.0, sq), axis=-1, keepdims=True)
    ms = jnp.where(low, s_lo, s_hi) * (1.0 / HEAD_DIM)
    return z * lax.rsqrt(ms + EPS) * g2


CAST_BLOCK_BYTES = 6 * 1024 * 1024


def _cast_kernel(x_ref, o_ref):
    o_ref[...] = x_ref[...].astype(o_ref.dtype)


def _to_bf16(w):
    depth, rows, cols = w.shape
    packing = 2 * SUBLANES
    fits = [r for r in range(packing, rows + 1, packing) if rows % r == 0 and r * cols * 4 <= CAST_BLOCK_BYTES]
    rb = max(fits)
    spec = pl.BlockSpec((1, rb, cols), lambda l, i: (l, i, 0))
    return pl.pallas_call(_cast_kernel, grid=(depth, rows // rb), in_specs=[spec], out_specs=spec,
                          out_shape=jax.ShapeDtypeStruct(w.shape, _BF16), compiler_params=_params(2),
                          name="cast")(w)


def _proj_kernel(x_ref, n1_ref, w_ref, bg_ref, gqa_ref, gka_ref, gqb_ref, gkb_ref, qa_ref, qb_ref, g_ref, *outs,
                 d_model, a_kv, emit_kv):
    ka_ref, va_ref, kb_ref, vb_ref = outs[:4] if emit_kv else (None,) * 4
    kat_ref, vat_ref, kbt_ref, vbt_ref = outs[-4:]
    o_ka = d_model
    o_va = o_ka + a_kv
    o_qb = o_va + a_kv
    o_kb = o_qb + d_model
    o_vb = o_kb + d_model
    o_g = o_vb + d_model

    h = _rmsnorm_rows(x_ref[0], n1_ref[...]).astype(_BF16)
    low = _lane_is_low()

    def slabs(base, width):
        for c in range(0, width, MXU_WIDTH):
            z = jnp.dot(h, w_ref[:, base + c:base + c + MXU_WIDTH], preferred_element_type=_F32)
            for t in range(MXU_WIDTH // LANES):
                yield c // LANES + t, z[:, t * LANES:(t + 1) * LANES]

    def emit(out_ref, base, width, gain_ref=None, tail_ref=None, twice=False, feature_major=False):
        def put(idx, y):
            out_ref[0, idx] = (y.T if feature_major else y).astype(out_ref.dtype)

        for s, y in slabs(base, width):
            if gain_ref is not None:
                y = _headnorm_slab(y, gain_ref[...])
            if tail_ref is not None:
                tail_ref[0, :, s * LANES:(s + 1) * LANES] = y
            if out_ref is None:
                continue
            if twice:
                y_sw = pltpu.roll(y, HEAD_DIM, 1)
                put(2 * s, jnp.where(low, y, y_sw))
                put(2 * s + 1, jnp.where(low, y_sw, y))
            else:
                put(s, y)

    emit(qa_ref, 0, d_model, gqa_ref)
    emit(ka_ref, o_ka, a_kv, gka_ref, kat_ref, twice=True, feature_major=True)
    emit(va_ref, o_va, a_kv, None, vat_ref, twice=True)
    emit(qb_ref, o_qb, d_model, gqb_ref)
    emit(kb_ref, o_kb, d_model, gkb_ref, kbt_ref, feature_major=True)
    emit(vb_ref, o_vb, d_model, None, vbt_ref)
    for s, y in slabs(o_g, 2 * d_model):
        c = s * LANES
        g_ref[0, s] = jax.nn.sigmoid(y + bg_ref[:, c:c + LANES]).astype(g_ref.dtype)


def _proj(x, n1, w_in, b_gate, gqa, gka, gqb, gkb, *, layer, tile, emit_kv=True):
    bsz, seq, d_model = x.shape
    a_kv = d_model // A_GROUP
    ns_d, ns_kv = d_model // LANES, a_kv // LANES

    def row(b, j):
        return (b, j, 0)

    def slab_row(b, j):
        return (b, 0, j, 0)

    def slab_col(b, j):
        return (b, 0, 0, j)

    def const2(b, j):
        return (0, 0)

    def tail(b, j):
        return (b, 0, 0)

    bf = lambda ns: jax.ShapeDtypeStruct((bsz, ns, seq, LANES), _BF16)
    bf_t = lambda ns: jax.ShapeDtypeStruct((bsz, ns, LANES, seq), _BF16)
    f32 = lambda ns: jax.ShapeDtypeStruct((bsz, tile, ns * LANES), _F32)
    blk = lambda ns: pl.BlockSpec((1, ns, tile, LANES), slab_row)
    blk_t = lambda ns: pl.BlockSpec((1, ns, LANES, tile), slab_col)
    blk_tail = lambda ns: pl.BlockSpec((1, tile, ns * LANES), tail)
    out_shape = [bf(ns_d), bf(ns_d), bf(2 * ns_d)]
    out_specs = [blk(ns_d), blk(ns_d), blk(2 * ns_d)]
    if emit_kv:
        out_shape += [bf_t(2 * ns_kv), bf(2 * ns_kv), bf_t(ns_d), bf(ns_d)]
        out_specs += [blk_t(2 * ns_kv), blk(2 * ns_kv), blk_t(ns_d), blk(ns_d)]
    out_shape += [f32(ns_kv), f32(ns_kv), f32(ns_d), f32(ns_d)]
    out_specs += [blk_tail(ns_kv), blk_tail(ns_kv), blk_tail(ns_d), blk_tail(ns_d)]
    in_specs = [pl.BlockSpec((1, tile, d_model), row),
                _resident((1, d_model), const2),
                _layer_resident(w_in, layer),
                _resident((1, 2 * d_model), const2),
                _resident((1, LANES), const2), _resident((1, LANES), const2),
                _resident((1, LANES), const2), _resident((1, LANES), const2)]
    kern = functools.partial(_proj_kernel, d_model=d_model, a_kv=a_kv, emit_kv=emit_kv)
    return pl.pallas_call(kern, grid=(bsz, seq // tile), in_specs=in_specs, out_specs=tuple(out_specs),
                          out_shape=tuple(out_shape), compiler_params=_params(2), name="proj")(
                              x, n1, w_in, b_gate, gqa, gka, gqb, gkb)


def _band_mask(tq, w, n_prev):
    shift = CHUNK.bit_length() - 1
    qc = lax.shift_right_logical(lax.broadcasted_iota(jnp.int32, (tq, w), 0), shift)
    kc = lax.shift_right_logical(lax.broadcasted_iota(jnp.int32, (tq, w), 1), shift)
    d = kc - qc
    return (d >= 0) & (d <= n_prev)


def _window(reach, tq):
    return -(-(reach + tq) // LANES) * LANES


def _stack_heads(qs):
    low = _lane_is_low()
    zero = jnp.zeros_like(qs)
    return jnp.concatenate([jnp.where(low, qs, zero), jnp.where(low, zero, qs)], axis=0)


def _with_ones(v):
    return jnp.concatenate([v, jnp.ones_like(v)], axis=1)


def _normalised(o2, extra_den, rows):
    den = o2[:, LANES:]
    if extra_den is not None:
        den = den + extra_den
    o = o2[:, :LANES] / den
    return jnp.where(_lane_is_low(), o[:rows], o[rows:])


def _init_bias(slopes_ref, rel_ref, bias_a, bias_b, ts):
    n_slabs, _, wa = bias_a.shape
    wb = bias_b.shape[2]
    qpos = lax.broadcasted_iota(jnp.int32, (ts, wa), 0)
    kpos = lax.broadcasted_iota(jnp.int32, (ts, wa), 1) - A_REACH
    dist = jnp.abs(qpos - kpos).astype(_F32)
    band_a = _band_mask(ts, wa, A_PREV)
    band_b = _band_mask(ts, wb, B_PREV)
    for h in range(2 * n_slabs):
        rows_h = slice((h % 2) * ts, (h % 2 + 1) * ts)
        bias_a[h // 2, rows_h, :] = jnp.where(band_a, (-LOG2E * slopes_ref[h]) * dist, NEG_INF)
        rows = jnp.broadcast_to(rel_ref[h:h + 1, :], (ts, REL_ROW))
        rows = pltpu.roll(rows, REL_ROW - ATTN_SUB + 1, 1, stride=1, stride_axis=0)
        bias_b[h // 2, rows_h, :] = jnp.where(band_b, LOG2E * rows[:, :wb], NEG_INF)


def _attn_kernel(*refs, tq, ts, n_blocks, n_slabs):
    na, nb = n_blocks
    slopes_ref, sinks_ref, rel_ref, qa_ref, qb_ref, g_ref = refs[:6]
    kv = list(refs[6:6 + 2 * (na + nb)])
    ka_refs, va_refs, kb_refs, vb_refs = kv[:na], kv[na:2 * na], kv[2 * na:2 * na + nb], kv[2 * na + nb:]
    o_ref, bias_a, bias_b = refs[6 + len(kv):9 + len(kv)]
    bufs = refs[9 + len(kv):]
    sa_ref, sb_ref, pa_ref, pb_ref, da_ref = (bufs[2 * n:2 * n + 2] for n in range(5))
    wa, wb = bias_a.shape[2], bias_b.shape[2]
    b = pl.program_id(0)
    i = pl.program_id(1)

    @pl.when((b == 0) & (i == 0))
    def _():
        _init_bias(slopes_ref, rel_ref, bias_a, bias_b, ts)

    row_is_lo = lax.broadcasted_iota(jnp.int32, (2 * ts, 1), 0) < ts
    col_a = lax.broadcasted_iota(jnp.int32, (1, wa), 1)
    col_b = lax.broadcasted_iota(jnp.int32, (1, wb), 1)

    def window(blocks, slab, u, reach, feature_major):
        if feature_major:
            cut = lambda r, lo, hi: r[0, slab, :, lo:hi]
        else:
            cut = lambda r, lo, hi: r[0, slab, lo:hi, :]
        lo, hi = u * ts - reach, (u + 1) * ts
        pieces = []
        for j, r in enumerate(blocks):
            base = (j - len(blocks) + 1) * tq
            if max(lo, base) < min(hi, base + tq):
                pieces.append(cut(r, max(lo, base) - base, min(hi, base + tq) - base))
        return pieces[0] if len(pieces) == 1 else jnp.concatenate(pieces, axis=1 if feature_major else 0)

    def softmax_weights(s, sink_col):
        m = jnp.max(s, axis=-1, keepdims=True)
        if sink_col is not None:
            m = jnp.maximum(m, sink_col)
        p = jnp.exp2(s - m).astype(_BF16)
        return p, None if sink_col is None else jnp.broadcast_to(jnp.exp2(sink_col - m), (2 * ts, LANES))

    def run(masked):
        valid = []
        for u in range(tq // ts):
            if masked:
                start = i * tq + u * ts
                valid.append((jnp.where(col_a >= A_REACH - start, 0.0, NEG_INF),
                              jnp.where(col_b >= B_REACH - start, 0.0, NEG_INF)))
            else:
                valid.append((None, None))

        def scores(k, u, par):
            rows_u = slice(u * ts, (u + 1) * ts)
            sa_ref[par][...] = jnp.dot(_stack_heads(qa_ref[0, k, rows_u]), window(ka_refs, k // 2, u, A_REACH, True),
                                       preferred_element_type=_F32)
            sb_ref[par][...] = jnp.dot(_stack_heads(qb_ref[0, k, rows_u]), window(kb_refs, k, u, B_REACH, True),
                                       preferred_element_type=_F32)

        def softmax(k, u, par):
            valid_a, valid_b = valid[u]
            sink_col = LOG2E * jnp.where(row_is_lo, sinks_ref[2 * k], sinks_ref[2 * k + 1])
            s = sa_ref[par][...] + bias_a[k]
            pa_ref[par][...], da_ref[par][...] = softmax_weights(s if valid_a is None else s + valid_a, sink_col)
            s = sb_ref[par][...] + bias_b[k]
            pb_ref[par][...], _ = softmax_weights(s if valid_b is None else s + valid_b, None)

        def values(k, u, par):
            rows_u = slice(u * ts, (u + 1) * ts)
            oa = jnp.dot(pa_ref[par][...], _with_ones(window(va_refs, k // 2, u, A_REACH, False)),
                         preferred_element_type=_F32)
            ob = jnp.dot(pb_ref[par][...], _with_ones(window(vb_refs, k, u, B_REACH, False)),
                         preferred_element_type=_F32)
            mixed = (g_ref[0, k, rows_u].astype(_F32) * _normalised(oa, da_ref[par][...], ts)
                     + g_ref[0, n_slabs + k, rows_u].astype(_F32) * _normalised(ob, None, ts))
            o_ref[0, k, rows_u] = mixed.astype(o_ref.dtype)

        scores(0, 0, 0)
        softmax(0, 0, 0)
        scores(0, 1, 1)

        def body(k, carry):
            values(k - 1, 0, 0)
            softmax(k - 1, 1, 1)
            scores(k, 0, 0)
            values(k - 1, 1, 1)
            softmax(k, 0, 0)
            scores(k, 1, 1)
            return carry

        lax.fori_loop(1, n_slabs, body, 0)
        values(n_slabs - 1, 0, 0)
        softmax(n_slabs - 1, 1, 1)
        values(n_slabs - 1, 1, 1)

    n_early = B_REACH // tq
    pl.when(i < n_early)(lambda: run(True))
    pl.when(i >= n_early)(lambda: run(False))


def _attention(slopes, sinks, rel_rows, qa, qb, gates, ka, va, kb, vb):
    bsz, n_slabs, seq, _ = qa.shape
    tq, ts = ATTN_TILE, ATTN_SUB
    assert tq == 2 * ts and seq % tq == 0 and B_REACH % tq == 0
    n_blocks = (1 + -(-A_REACH // tq), 1 + B_REACH // tq)

    def row(b, i):
        return (b, 0, i, 0)

    def const2(b, i):
        return (0, 0)

    def kv_specs(t, n, feature_major):
        if feature_major:
            return [pl.BlockSpec((1, t.shape[1], LANES, tq), lambda b, i, d=d: (b, 0, 0, jnp.maximum(i - d, 0)))
                    for d in range(n - 1, -1, -1)]
        return [pl.BlockSpec((1, t.shape[1], tq, LANES), lambda b, i, d=d: (b, 0, jnp.maximum(i - d, 0), 0))
                for d in range(n - 1, -1, -1)]

    smem = pl.BlockSpec(memory_space=pltpu.SMEM)
    rows = lambda ns: pl.BlockSpec((1, ns, tq, LANES), row)
    na, nb = n_blocks
    wa, wb = A_REACH + ts, B_REACH + ts
    in_specs = ([smem, smem, _resident(rel_rows.shape, const2), rows(n_slabs), rows(n_slabs), rows(2 * n_slabs)]
                + kv_specs(ka, na, True) + kv_specs(va, na, False) + kv_specs(kb, nb, True) + kv_specs(vb, nb, False))
    operands = [slopes, sinks, rel_rows, qa, qb, gates] + [ka] * na + [va] * na + [kb] * nb + [vb] * nb
    kern = functools.partial(_attn_kernel, tq=tq, ts=ts, n_blocks=n_blocks, n_slabs=n_slabs)
    return pl.pallas_call(
        kern, grid=(bsz, seq // tq), in_specs=in_specs, out_specs=rows(n_slabs),
        out_shape=jax.ShapeDtypeStruct(qa.shape, _BF16),
        scratch_shapes=[pltpu.VMEM((n_slabs, 2 * ts, wa), _F32), pltpu.VMEM((n_slabs, 2 * ts, wb), _F32)]
        + [pltpu.VMEM((2 * ts, w), dt)
           for w, dt in ((wa, _F32), (wb, _F32), (wa, _BF16), (wb, _BF16), (LANES, _F32)) for _ in range(2)],
        compiler_params=_params(2), name="attn")(*operands)


def _sample_attn_kernel(slopes_ref, sinks_ref, rel_ref, qa_ref, qb_ref, g_ref,
                        cak_ref, cav_ref, cbk_ref, cbv_ref, nak_ref, nav_ref, nbk_ref, nbv_ref,
                        o_ref, bias_a, bias_b, *, tq, n_slabs):
    @pl.when(pl.program_id(0) == 0)
    def _():
        _init_bias(slopes_ref, rel_ref, bias_a, bias_b, tq)

    low = _lane_is_low()
    row_is_lo = lax.broadcasted_iota(jnp.int32, (2 * tq, 1), 0) < tq

    def both_halves(x, half):
        sw = pltpu.roll(x, HEAD_DIM, 1)
        return (jnp.where(low, x, sw) if half == 0 else jnp.where(low, sw, x)).astype(_BF16)

    def attend(qs, parts, bias, sink_col):
        qq = _stack_heads(qs)
        ss = [lax.dot_general(qq, k, (((1,), (1,)), ((), ())), preferred_element_type=_F32)
              + bias[:, c0:c0 + k.shape[0]] for k, _, c0 in parts]
        m = functools.reduce(jnp.maximum, [jnp.max(s, axis=-1, keepdims=True) for s in ss])
        if sink_col is not None:
            m = jnp.maximum(m, sink_col)
        o2 = sum(jnp.dot(jnp.exp2(s - m).astype(_BF16), _with_ones(v), preferred_element_type=_F32)
                 for s, (_, v, _) in zip(ss, parts))
        return _normalised(o2, None if sink_col is None else jnp.exp2(sink_col - m), tq)

    for s in range(n_slabs):
        kvh = s // 2
        la = slice(kvh // 2 * LANES, (kvh // 2 + 1) * LANES)
        dup = lambda r: both_halves(r[0, :, la], kvh % 2)
        sink_col = LOG2E * jnp.where(row_is_lo, sinks_ref[2 * s], sinks_ref[2 * s + 1])
        oa = attend(qa_ref[0, s], [(dup(cak_ref), dup(cav_ref), 0), (dup(nak_ref), dup(nav_ref), A_REACH)],
                    bias_a[s], sink_col)
        lb = slice(s * LANES, (s + 1) * LANES)
        cut = lambda r: r[0, :, lb].astype(_BF16)
        ob = attend(qb_ref[0, s], [(cut(cbk_ref), cut(cbv_ref), 0), (cut(nbk_ref), cut(nbv_ref), B_REACH)],
                    bias_b[s], None)
        mixed = g_ref[0, s].astype(_F32) * oa + g_ref[0, n_slabs + s].astype(_F32) * ob
        o_ref[0, s] = mixed.astype(o_ref.dtype)


def _sample_attention(slopes, sinks, rel_rows, qa, qb, gates, cache_a_k, cache_a_v, cache_b_k, cache_b_v,
                      new_a_k, new_a_v, new_b_k, new_b_v, *, layer, tq):
    _, n_slabs, rows, _ = qa.shape
    n_batches = rows // tq

    def q_rows(b):
        return (0, 0, b, 0)

    def cached(b):
        return (layer * n_batches + b, 0, 0)

    def new_rows(b):
        return (0, b, 0)

    smem = pl.BlockSpec(memory_space=pltpu.SMEM)
    q_blk = lambda ns: pl.BlockSpec((1, ns, tq, LANES), q_rows)
    cache_blk = lambda t: pl.BlockSpec((1,) + t.shape[1:], cached)
    new_blk = lambda t: pl.BlockSpec((1, tq, t.shape[2]), new_rows)
    caches = (cache_a_k, cache_a_v, cache_b_k, cache_b_v)
    news = (new_a_k, new_a_v, new_b_k, new_b_v)
    in_specs = ([smem, smem, _resident(rel_rows.shape, lambda b: (0, 0)), q_blk(n_slabs), q_blk(n_slabs),
                 q_blk(2 * n_slabs)] + [cache_blk(t) for t in caches] + [new_blk(t) for t in news])
    kern = functools.partial(_sample_attn_kernel, tq=tq, n_slabs=n_slabs)
    return pl.pallas_call(
        kern, grid=(n_batches,), in_specs=in_specs, out_specs=q_blk(n_slabs),
        out_shape=jax.ShapeDtypeStruct(qa.shape, _BF16),
        scratch_shapes=[pltpu.VMEM((n_slabs, 2 * tq, _window(A_REACH, tq)), _F32),
                        pltpu.VMEM((n_slabs, 2 * tq, _window(B_REACH, tq)), _F32)],
        compiler_params=_params(1), name="sample_attn")(slopes, sinks, rel_rows, qa, qb, gates, *caches, *news)


def _gelu(x):
    return 0.5 * x * (1.0 + lax.erf(x * (2.0 ** -0.5)))


def _ffn_kernel(*refs, d_ff, seg, carried):
    if carried:
        x_ref, m_ref, wo_ref, n2_ref, wu_ref, cw_ref, cb_ref, wd_ref, y_ref, st_ref, carry, act = refs
    else:
        x_ref, m_ref, wo_ref, n2_ref, wu_ref, cw_ref, cb_ref, wd_ref, prev_ref, y_ref, st_ref, act = refs
    rows = x_ref.shape[1]
    n_seg = rows // seg

    if carried:
        @pl.when(pl.program_id(1) == 0)
        def _():
            carry[...] = jnp.zeros(carry.shape, carry.dtype)

    mixed = jnp.concatenate([m_ref[0, s] for s in range(m_ref.shape[1])], axis=-1)
    x1 = x_ref[0] + jnp.dot(mixed, wo_ref[...], preferred_element_type=_F32)
    h = _rmsnorm_rows(x1, n2_ref[...]).astype(_BF16)
    r = jnp.bitwise_and(lax.broadcasted_iota(jnp.int32, (rows, 1), 0), seg - 1)

    def conv(col):
        u = jnp.dot(h, wu_ref[:, col:col + FF_CHUNK], preferred_element_type=_F32)
        cs = slice(col, col + FF_CHUNK)
        if carried:
            p0, p1 = carry[SUBLANES - 2:SUBLANES - 1, cs], carry[SUBLANES - 1:SUBLANES, cs]
            carry[:, cs] = u[rows - SUBLANES:, :]
        else:
            per_seg = lambda j: jnp.concatenate(
                [jnp.broadcast_to(prev_ref[g, j:j + 1, cs], (seg, FF_CHUNK)) for g in range(n_seg)], axis=0)
            p0, p1 = per_seg(0), per_seg(1)
        for g in range(n_seg):
            st_ref[g, :, cs] = u[(g + 1) * seg - SUBLANES:(g + 1) * seg, :]
        u1 = jnp.where(r == 0, p1, pltpu.roll(u, 1, 0))
        u2 = jnp.where(r == 0, p0, jnp.where(r == 1, p1, pltpu.roll(u, 2, 0)))
        return cb_ref[:, cs] + cw_ref[0:1, cs] * u2 + cw_ref[1:2, cs] * u1 + cw_ref[2:3, cs] * u

    for c in range(0, d_ff, FF_CHUNK):
        a = conv(c)
        g = conv(d_ff + c)
        act[:, c:c + FF_CHUNK] = (_gelu(a) * g).astype(act.dtype)

    y_ref[0] = x1 + jnp.dot(act[...], wd_ref[...], preferred_element_type=_F32)


def _ffn(x, mixed, w_out, n2, w_up, conv_w, conv_b, w_down, prev=None, *, layer, tile, seg):
    bsz, seq, d_model = x.shape
    d_ff = w_down.shape[1]
    n_tiles = seq // tile
    carried = prev is None
    n_seg = 1 if carried else tile // seg
    assert seg & (seg - 1) == 0 and (seg == seq if carried else tile % seg == 0)

    def row(b, i):
        return (b, i, 0)

    def slab_row(b, i):
        return (b, 0, i, 0)

    def segs(b, i):
        return (b if carried else b * n_tiles + i, 0, 0)

    def const2(b, i):
        return (0, 0)

    operands = [x, mixed, w_out, n2, w_up, conv_w, conv_b, w_down]
    in_specs = [pl.BlockSpec((1, tile, d_model), row),
                pl.BlockSpec((1, mixed.shape[1], tile, LANES), slab_row),
                _layer_resident(w_out, layer), _resident((1, d_model), const2),
                _layer_resident(w_up, layer), _resident(conv_w.shape, const2),
                _resident((1, 2 * d_ff), const2), _layer_resident(w_down, layer)]
    scratch = [pltpu.VMEM((tile, d_ff), _BF16)]
    if carried:
        scratch.insert(0, pltpu.VMEM((SUBLANES, 2 * d_ff), _F32))
    else:
        operands.append(prev)
        in_specs.append(pl.BlockSpec((n_seg,) + prev.shape[1:], segs))
    n_state = bsz * (1 if carried else n_tiles * n_seg)
    out_specs = (pl.BlockSpec((1, tile, d_model), row), pl.BlockSpec((n_seg, SUBLANES, 2 * d_ff), segs))
    out_shape = (jax.ShapeDtypeStruct((bsz, seq, d_model), _F32),
                 jax.ShapeDtypeStruct((n_state, SUBLANES, 2 * d_ff), _F32))
    kern = functools.partial(_ffn_kernel, d_ff=d_ff, seg=tile if carried else seg, carried=carried)
    return pl.pallas_call(
        kern, grid=(bsz, n_tiles), in_specs=in_specs, out_specs=out_specs, out_shape=out_shape,
        scratch_shapes=scratch, compiler_params=_params(2), name="ffn")(*operands)


def _rel_rows(table):
    n_low = B_REACH - REL_CLIP + ATTN_SUB - 1
    n_high = REL_ROW - n_low - table.shape[0]
    rows = jnp.concatenate([jnp.broadcast_to(table[-1:], (n_low, table.shape[1])),
                            table[::-1],
                            jnp.broadcast_to(table[:1], (n_high, table.shape[1]))], axis=0)
    return rows.T


def _tile2(g, scale=1.0):
    return (jnp.tile(g, 2) * scale).reshape(1, LANES)


def kernel(x_prompt, x_sample, cache_a_k, cache_a_v, cache_b_k, cache_b_v, cache_ffn_conv, norm1_g, w_in, b_gate, qn_a_g, kn_a_g, qn_b_g, kn_b_g, sinks_a, rel_bias_b, w_out, norm2_g, w_up, conv_w, conv_b, w_down):
    depth = w_in.shape[0]
    bsz, seq, d_model = x_prompt.shape
    dbs, dseq, _ = x_sample.shape
    n_heads = d_model // HEAD_DIM
    assert cache_a_k.shape[2] == A_REACH and cache_b_k.shape[2] == B_REACH
    assert seq % PROJ_TILE == 0 and PROJ_TILE >= B_REACH and dseq <= CHUNK

    slopes = 2.0 ** (-8.0 * jnp.arange(1, n_heads + 1, dtype=_F32) / n_heads)
    xp, xs = x_prompt, x_sample.reshape(1, dbs * dseq, d_model)
    heads = lambda t, rows: t.reshape(-1, rows, t.shape[-1] // HEAD_DIM, HEAD_DIM)
    w_in, w_out, w_up, w_down = (_to_bf16(w) for w in (w_in, w_out, w_up, w_down))
    flat = lambda c: c.reshape(depth * dbs, c.shape[2], -1)
    caches = [flat(c) for c in (cache_a_k, cache_a_v, cache_b_k, cache_b_v)]
    pk, sk = [], []
    for l in range(depth):
        n1, n2 = norm1_g[l].reshape(1, -1), norm2_g[l].reshape(1, -1)
        bg, cb = b_gate[l].reshape(1, -1), conv_b[l].reshape(1, -1)
        gains = (_tile2(qn_a_g[l], Q_SCALE), _tile2(kn_a_g[l]), _tile2(qn_b_g[l], Q_SCALE), _tile2(kn_b_g[l]))
        rel = _rel_rows(rel_bias_b[l])

        qa, qb, g, ka, va, kb, vb, kat, vat, kbt, vbt = _proj(
            xp, n1, w_in, bg, *gains, layer=l, tile=PROJ_TILE)
        mixed = _attention(slopes, sinks_a[l], rel, qa, qb, g, ka, va, kb, vb)
        xp, st = _ffn(xp, mixed, w_out, n2, w_up, conv_w[l], cb, w_down, layer=l, tile=PROJ_TILE, seg=seq)
        pk.append((heads(kat[:, -A_REACH:], A_REACH), heads(vat[:, -A_REACH:], A_REACH),
                   heads(kbt, B_REACH), heads(vbt, B_REACH), st[:, -(CONV_W - 1):]))

        qa, qb, g, kat, vat, kbt, vbt = _proj(
            xs, n1, w_in, bg, *gains, layer=l, tile=dbs * dseq, emit_kv=False)
        mixed = _sample_attention(slopes, sinks_a[l], rel, qa, qb, g, *caches, kat, vat, kbt, vbt, layer=l, tq=dseq)
        xs, st = _ffn(xs, mixed, w_out, n2, w_up, conv_w[l], cb, w_down, cache_ffn_conv[l],
                      layer=l, tile=dbs * dseq, seg=dseq)
        sk.append((heads(kat, dseq), heads(vat, dseq), heads(kbt, dseq), heads(vbt, dseq),
                   st[:, -(CONV_W - 1):]))

    stk = lambda states, i: jnp.stack([s[i] for s in states])
    return (xp, xs.reshape(dbs, dseq, d_model),
            stk(pk, 0), stk(pk, 1), stk(pk, 2), stk(pk, 3), stk(pk, 4),
            stk(sk, 0), stk(sk, 1), stk(sk, 2), stk(sk, 3), stk(sk, 4))
```

```python
import functools

import jax
import jax.numpy as jnp
from jax import lax
from jax.experimental import pallas as pl
from jax.experimental.pallas import tpu as pltpu

HEAD_DIM = 64
CHUNK = 64
A_GROUP = 4
A_PREV = 2
B_PREV = 8
A_REACH = A_PREV * CHUNK
B_REACH = B_PREV * CHUNK
REL_CLIP = 128
CONV_W = 3
EPS = 1e-6
NEG_INF = -1e30
LOG2E = 1.4426950408889634
Q_SCALE = HEAD_DIM ** -0.5 * LOG2E

LANES = 128
SUBLANES = 8
MXU_WIDTH = 256
VMEM_LIMIT_BYTES = 56 * 1024 * 1024

PROJ_TILE = 512
ATTN_TILE = 256
ATTN_SUB = 128
FF_CHUNK = 256
REL_ROW = 1024

_BF16 = jnp.bfloat16
_F32 = jnp.float32


def _resident(shape, index_map):
    return pl.BlockSpec(shape, index_map, pipeline_mode=pl.Buffered(1))


def _layer_resident(w, layer):
    return pl.BlockSpec((None,) + w.shape[1:], lambda *_: (layer, 0, 0), pipeline_mode=pl.Buffered(1))


def _params(n_axes, flags=None):
    return pltpu.CompilerParams(dimension_semantics=("arbitrary",) * n_axes,
                                vmem_limit_bytes=VMEM_LIMIT_BYTES, flags=flags)


def _rmsnorm_rows(x, g):
    return x * lax.rsqrt(jnp.mean(x * x, axis=-1, keepdims=True) + EPS) * g


def _lane_is_low():
    return lax.broadcasted_iota(jnp.int32, (1, LANES), 1) < HEAD_DIM


def _headnorm_slab(z, g2):
    low = _lane_is_low()
    sq = z * z
    s_lo = jnp.sum(jnp.where(low, sq, 0.0), axis=-1, keepdims=True)
    s_hi = jnp.sum(jnp.where(low, 0.0, sq), axis=-1, keepdims=True)
    ms = jnp.where(low, s_lo, s_hi) * (1.0 / HEAD_DIM)
    return z * lax.rsqrt(ms + EPS) * g2


CAST_BLOCK_BYTES = 6 * 1024 * 1024


def _cast_kernel(x_ref, o_ref):
    o_ref[...] = x_ref[...].astype(o_ref.dtype)


def _to_bf16(w):
    depth, rows, cols = w.shape
    packing = 2 * SUBLANES
    fits = [r for r in range(packing, rows + 1, packing) if rows % r == 0 and r * cols * 4 <= CAST_BLOCK_BYTES]
    rb = max(fits)
    spec = pl.BlockSpec((1, rb, cols), lambda l, i: (l, i, 0))
    return pl.pallas_call(_cast_kernel, grid=(depth, rows // rb), in_specs=[spec], out_specs=spec,
                          out_shape=jax.ShapeDtypeStruct(w.shape, _BF16), compiler_params=_params(2),
                          name="cast")(w)


def _proj_kernel(x_ref, n1_ref, w_ref, bg_ref, gqa_ref, gka_ref, gqb_ref, gkb_ref, qa_ref, qb_ref, g_ref, *outs,
                 d_model, a_kv, emit_kv):
    ka_ref, va_ref, kb_ref, vb_ref = outs[:4] if emit_kv else (None,) * 4
    kat_ref, vat_ref, kbt_ref, vbt_ref = outs[-4:]
    o_ka = d_model
    o_va = o_ka + a_kv
    o_qb = o_va + a_kv
    o_kb = o_qb + d_model
    o_vb = o_kb + d_model
    o_g = o_vb + d_model

    h = _rmsnorm_rows(x_ref[0], n1_ref[...]).astype(_BF16)
    low = _lane_is_low()

    def slabs(base, width):
        for c in range(0, width, MXU_WIDTH):
            z = jnp.dot(h, w_ref[:, base + c:base + c + MXU_WIDTH], preferred_element_type=_F32)
            for t in range(MXU_WIDTH // LANES):
                yield c // LANES + t, z[:, t * LANES:(t + 1) * LANES]

    def emit(out_ref, base, width, gain_ref=None, tail_ref=None, twice=False, feature_major=False):
        def put(idx, y):
            if feature_major:
                yt = y.T.astype(out_ref.dtype)
                for t in range(out_ref.shape[1]):
                    out_ref[0, t, idx] = yt[:, t * ATTN_TILE:(t + 1) * ATTN_TILE]
            else:
                out_ref[0, idx] = y.astype(out_ref.dtype)

        for s, y in slabs(base, width):
            if gain_ref is not None:
                y = _headnorm_slab(y, gain_ref[...])
            if tail_ref is not None:
                tail_ref[0, :, s * LANES:(s + 1) * LANES] = y
            if out_ref is None:
                continue
            if twice:
                y_sw = pltpu.roll(y, HEAD_DIM, 1)
                put(2 * s, jnp.where(low, y, y_sw))
                put(2 * s + 1, jnp.where(low, y_sw, y))
            else:
                put(s, y)

    emit(qa_ref, 0, d_model, gqa_ref)
    emit(ka_ref, o_ka, a_kv, gka_ref, kat_ref, twice=True, feature_major=True)
    emit(va_ref, o_va, a_kv, None, vat_ref, twice=True)
    emit(qb_ref, o_qb, d_model, gqb_ref)
    emit(kb_ref, o_kb, d_model, gkb_ref, kbt_ref, feature_major=True)
    emit(vb_ref, o_vb, d_model, None, vbt_ref)
    for s, y in slabs(o_g, 2 * d_model):
        c = s * LANES
        g_ref[0, s] = jax.nn.sigmoid(y + bg_ref[:, c:c + LANES]).astype(g_ref.dtype)


def _proj(x, n1, w_in, b_gate, gqa, gka, gqb, gkb, *, layer, tile, emit_kv=True):
    bsz, seq, d_model = x.shape
    a_kv = d_model // A_GROUP
    ns_d, ns_kv = d_model // LANES, a_kv // LANES

    def row(b, j):
        return (b, j, 0)

    def slab_row(b, j):
        return (b, 0, j, 0)

    def key_blocks(b, j):
        return (b, j, 0, 0, 0)

    def const2(b, j):
        return (0, 0)

    def tail(b, j):
        return (b, 0, 0)

    bf = lambda ns: jax.ShapeDtypeStruct((bsz, ns, seq, LANES), _BF16)
    bf_t = lambda ns: jax.ShapeDtypeStruct((bsz, seq // ATTN_TILE, ns, LANES, ATTN_TILE), _BF16)
    f32 = lambda ns: jax.ShapeDtypeStruct((bsz, tile, ns * LANES), _F32)
    blk = lambda ns: pl.BlockSpec((1, ns, tile, LANES), slab_row)
    blk_t = lambda ns: pl.BlockSpec((1, tile // ATTN_TILE, ns, LANES, ATTN_TILE), key_blocks)
    blk_tail = lambda ns: pl.BlockSpec((1, tile, ns * LANES), tail)
    out_shape = [bf(ns_d), bf(ns_d), bf(2 * ns_d)]
    out_specs = [blk(ns_d), blk(ns_d), blk(2 * ns_d)]
    if emit_kv:
        out_shape += [bf_t(2 * ns_kv), bf(2 * ns_kv), bf_t(ns_d), bf(ns_d)]
        out_specs += [blk_t(2 * ns_kv), blk(2 * ns_kv), blk_t(ns_d), blk(ns_d)]
    out_shape += [f32(ns_kv), f32(ns_kv), f32(ns_d), f32(ns_d)]
    out_specs += [blk_tail(ns_kv), blk_tail(ns_kv), blk_tail(ns_d), blk_tail(ns_d)]
    in_specs = [pl.BlockSpec((1, tile, d_model), row),
                _resident((1, d_model), const2),
                _layer_resident(w_in, layer),
                _resident((1, 2 * d_model), const2),
                _resident((1, LANES), const2), _resident((1, LANES), const2),
                _resident((1, LANES), const2), _resident((1, LANES), const2)]
    kern = functools.partial(_proj_kernel, d_model=d_model, a_kv=a_kv, emit_kv=emit_kv)
    return pl.pallas_call(kern, grid=(bsz, seq // tile), in_specs=in_specs, out_specs=tuple(out_specs),
                          out_shape=tuple(out_shape), compiler_params=_params(2), name="proj")(
                              x, n1, w_in, b_gate, gqa, gka, gqb, gkb)


def _band_mask(tq, w, n_prev):
    shift = CHUNK.bit_length() - 1
    qc = lax.shift_right_logical(lax.broadcasted_iota(jnp.int32, (tq, w), 0), shift)
    kc = lax.shift_right_logical(lax.broadcasted_iota(jnp.int32, (tq, w), 1), shift)
    d = kc - qc
    return (d >= 0) & (d <= n_prev)


def _window(reach, tq):
    return -(-(reach + tq) // LANES) * LANES


def _stack_heads(qs):
    low = _lane_is_low()
    zero = jnp.zeros_like(qs)
    return jnp.concatenate([jnp.where(low, qs, zero), jnp.where(low, zero, qs)], axis=0)


def _with_ones(v):
    return jnp.concatenate([v, jnp.ones_like(v)], axis=1)


def _normalised(o2, extra_den, rows):
    den = o2[:, LANES:]
    if extra_den is not None:
        den = den + extra_den
    o = o2[:, :LANES] / den
    return jnp.where(_lane_is_low(), o[:rows], o[rows:])


def _init_bias(slopes_ref, rel_ref, bias_a, bias_b, ts):
    n_slabs, _, wa = bias_a.shape
    wb = bias_b.shape[2]
    qpos = lax.broadcasted_iota(jnp.int32, (ts, wa), 0)
    kpos = lax.broadcasted_iota(jnp.int32, (ts, wa), 1) - A_REACH
    dist = jnp.abs(qpos - kpos).astype(_F32)
    band_a = _band_mask(ts, wa, A_PREV)
    band_b = _band_mask(ts, wb, B_PREV)
    for h in range(2 * n_slabs):
        rows_h = slice((h % 2) * ts, (h % 2 + 1) * ts)
        bias_a[h // 2, rows_h, :] = jnp.where(band_a, (-LOG2E * slopes_ref[h]) * dist, NEG_INF)
        rows = jnp.broadcast_to(rel_ref[h:h + 1, :], (ts, REL_ROW))
        rows = pltpu.roll(rows, REL_ROW - ATTN_SUB + 1, 1, stride=1, stride_axis=0)
        bias_b[h // 2, rows_h, :] = jnp.where(band_b, LOG2E * rows[:, :wb], NEG_INF)


def _attn_kernel(*refs, tq, ts, n_blocks, n_slabs):
    na, nb = n_blocks
    slopes_ref, sinks_ref, rel_ref, qa_ref, qb_ref, g_ref = refs[:6]
    kv = list(refs[6:6 + 2 * (na + nb)])
    ka_refs, va_refs, kb_refs, vb_refs = kv[:na], kv[na:2 * na], kv[2 * na:2 * na + nb], kv[2 * na + nb:]
    o_ref, bias_a, bias_b = refs[6 + len(kv):9 + len(kv)]
    bufs = refs[9 + len(kv):]
    sa_ref, sb_ref, pa_ref, pb_ref, da_ref = (bufs[2 * n:2 * n + 2] for n in range(5))
    wa, wb = bias_a.shape[2], bias_b.shape[2]
    b = pl.program_id(0)
    i = pl.program_id(1)

    @pl.when((b == 0) & (i == 0))
    def _():
        _init_bias(slopes_ref, rel_ref, bias_a, bias_b, ts)

    row_is_lo = lax.broadcasted_iota(jnp.int32, (2 * ts, 1), 0) < ts
    col_a = lax.broadcasted_iota(jnp.int32, (1, wa), 1)
    col_b = lax.broadcasted_iota(jnp.int32, (1, wb), 1)

    def window(blocks, slab, u, reach, feature_major):
        if feature_major:
            cut = lambda r, lo, hi: r[0, 0, slab, :, lo:hi]
        else:
            cut = lambda r, lo, hi: r[0, slab, lo:hi, :]
        lo, hi = u * ts - reach, (u + 1) * ts
        pieces = []
        for j, r in enumerate(blocks):
            base = (j - len(blocks) + 1) * tq
            if max(lo, base) < min(hi, base + tq):
                pieces.append(cut(r, max(lo, base) - base, min(hi, base + tq) - base))
        return pieces[0] if len(pieces) == 1 else jnp.concatenate(pieces, axis=1 if feature_major else 0)

    def softmax_weights(s, sink_col):
        m = jnp.max(s, axis=-1, keepdims=True)
        if sink_col is not None:
            m = jnp.maximum(m, sink_col)
        p = jnp.exp2(s - m).astype(_BF16)
        return p, None if sink_col is None else jnp.broadcast_to(jnp.exp2(sink_col - m), (2 * ts, LANES))

    def run(masked):
        valid = []
        for u in range(tq // ts):
            if masked:
                start = i * tq + u * ts
                valid.append((jnp.where(col_a >= A_REACH - start, 0.0, NEG_INF),
                              jnp.where(col_b >= B_REACH - start, 0.0, NEG_INF)))
            else:
                valid.append((None, None))

        def scores(k, u, par):
            rows_u = slice(u * ts, (u + 1) * ts)
            sa_ref[par][...] = jnp.dot(_stack_heads(qa_ref[0, k, rows_u]), window(ka_refs, k // 2, u, A_REACH, True),
                                       preferred_element_type=_F32)
            sb_ref[par][...] = jnp.dot(_stack_heads(qb_ref[0, k, rows_u]), window(kb_refs, k, u, B_REACH, True),
                                       preferred_element_type=_F32)

        def softmax(k, u, par):
            valid_a, valid_b = valid[u]
            sink_col = LOG2E * jnp.where(row_is_lo, sinks_ref[2 * k], sinks_ref[2 * k + 1])
            s = sa_ref[par][...] + bias_a[k]
            pa_ref[par][...], da_ref[par][...] = softmax_weights(s if valid_a is None else s + valid_a, sink_col)
            s = sb_ref[par][...] + bias_b[k]
            pb_ref[par][...], _ = softmax_weights(s if valid_b is None else s + valid_b, None)

        def values(k, u, par):
            rows_u = slice(u * ts, (u + 1) * ts)
            oa = jnp.dot(pa_ref[par][...], _with_ones(window(va_refs, k // 2, u, A_REACH, False)),
                         preferred_element_type=_F32)
            ob = jnp.dot(pb_ref[par][...], _with_ones(window(vb_refs, k, u, B_REACH, False)),
                         preferred_element_type=_F32)
            mixed = (g_ref[0, k, rows_u].astype(_F32) * _normalised(oa, da_ref[par][...], ts)
                     + g_ref[0, n_slabs + k, rows_u].astype(_F32) * _normalised(ob, None, ts))
            o_ref[0, k, rows_u] = mixed.astype(o_ref.dtype)

        scores(0, 0, 0)
        softmax(0, 0, 0)
        scores(0, 1, 1)

        def body(k, carry):
            values(k - 1, 0, 0)
            softmax(k - 1, 1, 1)
            scores(k, 0, 0)
            values(k - 1, 1, 1)
            softmax(k, 0, 0)
            scores(k, 1, 1)
            return carry

        lax.fori_loop(1, n_slabs, body, 0)
        values(n_slabs - 1, 0, 0)
        softmax(n_slabs - 1, 1, 1)
        values(n_slabs - 1, 1, 1)

    n_early = B_REACH // tq
    pl.when(i < n_early)(lambda: run(True))
    pl.when(i >= n_early)(lambda: run(False))


def _attention(slopes, sinks, rel_rows, qa, qb, gates, ka, va, kb, vb):
    bsz, n_slabs, seq, _ = qa.shape
    tq, ts = ATTN_TILE, ATTN_SUB
    assert tq == 2 * ts and seq % tq == 0 and B_REACH % tq == 0
    n_blocks = (1 + -(-A_REACH // tq), 1 + B_REACH // tq)

    def row(b, i):
        return (b, 0, i, 0)

    def const2(b, i):
        return (0, 0)

    def kv_specs(t, n, feature_major):
        if feature_major:
            return [pl.BlockSpec((1, 1) + t.shape[2:], lambda b, i, d=d: (b, jnp.maximum(i - d, 0), 0, 0, 0))
                    for d in range(n - 1, -1, -1)]
        return [pl.BlockSpec((1, t.shape[1], tq, LANES), lambda b, i, d=d: (b, 0, jnp.maximum(i - d, 0), 0))
                for d in range(n - 1, -1, -1)]

    smem = pl.BlockSpec(memory_space=pltpu.SMEM)
    rows = lambda ns: pl.BlockSpec((1, ns, tq, LANES), row)
    na, nb = n_blocks
    wa, wb = A_REACH + ts, B_REACH + ts
    in_specs = ([smem, smem, _resident(rel_rows.shape, const2), rows(n_slabs), rows(n_slabs), rows(2 * n_slabs)]
                + kv_specs(ka, na, True) + kv_specs(va, na, False) + kv_specs(kb, nb, True) + kv_specs(vb, nb, False))
    operands = [slopes, sinks, rel_rows, qa, qb, gates] + [ka] * na + [va] * na + [kb] * nb + [vb] * nb
    kern = functools.partial(_attn_kernel, tq=tq, ts=ts, n_blocks=n_blocks, n_slabs=n_slabs)
    return pl.pallas_call(
        kern, grid=(bsz, seq // tq), in_specs=in_specs, out_specs=rows(n_slabs),
        out_shape=jax.ShapeDtypeStruct(qa.shape, _BF16),
        scratch_shapes=[pltpu.VMEM((n_slabs, 2 * ts, wa), _F32), pltpu.VMEM((n_slabs, 2 * ts, wb), _F32)]
        + [pltpu.VMEM((2 * ts, w), dt)
           for w, dt in ((wa, _F32), (wb, _F32), (wa, _BF16), (wb, _BF16), (LANES, _F32)) for _ in range(2)],
        compiler_params=_params(2), name="attn")(*operands)


def _sample_attn_kernel(slopes_ref, sinks_ref, rel_ref, qa_ref, qb_ref, g_ref,
                        cak_ref, cav_ref, cbk_ref, cbv_ref, nak_ref, nav_ref, nbk_ref, nbv_ref,
                        o_ref, bias_a, bias_b, *, tq, n_slabs):
    @pl.when(pl.program_id(0) == 0)
    def _():
        _init_bias(slopes_ref, rel_ref, bias_a, bias_b, tq)

    low = _lane_is_low()
    row_is_lo = lax.broadcasted_iota(jnp.int32, (2 * tq, 1), 0) < tq

    def both_halves(x, half):
        sw = pltpu.roll(x, HEAD_DIM, 1)
        return (jnp.where(low, x, sw) if half == 0 else jnp.where(low, sw, x)).astype(_BF16)

    def attend(qs, parts, bias, sink_col):
        qq = _stack_heads(qs)
        ss = [lax.dot_general(qq, k, (((1,), (1,)), ((), ())), preferred_element_type=_F32)
              + bias[:, c0:c0 + k.shape[0]] for k, _, c0 in parts]
        m = functools.reduce(jnp.maximum, [jnp.max(s, axis=-1, keepdims=True) for s in ss])
        if sink_col is not None:
            m = jnp.maximum(m, sink_col)
        o2 = sum(jnp.dot(jnp.exp2(s - m).astype(_BF16), _with_ones(v), preferred_element_type=_F32)
                 for s, (_, v, _) in zip(ss, parts))
        return _normalised(o2, None if sink_col is None else jnp.exp2(sink_col - m), tq)

    for s in range(n_slabs):
        kvh = s // 2
        la = slice(kvh // 2 * LANES, (kvh // 2 + 1) * LANES)
        dup = lambda r: both_halves(r[0, :, la].astype(_F32), kvh % 2)
        sink_col = LOG2E * jnp.where(row_is_lo, sinks_ref[2 * s], sinks_ref[2 * s + 1])
        oa = attend(qa_ref[0, s], [(dup(cak_ref), dup(cav_ref), 0), (dup(nak_ref), dup(nav_ref), A_REACH)],
                    bias_a[s], sink_col)
        lb = slice(s * LANES, (s + 1) * LANES)
        cut = lambda r: r[0, :, lb].astype(_BF16)
        ob = attend(qb_ref[0, s], [(cut(cbk_ref), cut(cbv_ref), 0), (cut(nbk_ref), cut(nbv_ref), B_REACH)],
                    bias_b[s], None)
        mixed = g_ref[0, s].astype(_F32) * oa + g_ref[0, n_slabs + s].astype(_F32) * ob
        o_ref[0, s] = mixed.astype(o_ref.dtype)


def _sample_attention(slopes, sinks, rel_rows, qa, qb, gates, cache_a_k, cache_a_v, cache_b_k, cache_b_v,
                      new_a_k, new_a_v, new_b_k, new_b_v, *, layer, tq):
    _, n_slabs, rows, _ = qa.shape
    n_batches = rows // tq

    def q_rows(b):
        return (0, 0, b, 0)

    def cached(b):
        return (layer * n_batches + b, 0, 0)

    def new_rows(b):
        return (0, b, 0)

    smem = pl.BlockSpec(memory_space=pltpu.SMEM)
    q_blk = lambda ns: pl.BlockSpec((1, ns, tq, LANES), q_rows)
    cache_blk = lambda t: pl.BlockSpec((1,) + t.shape[1:], cached)
    new_blk = lambda t: pl.BlockSpec((1, tq, t.shape[2]), new_rows)
    caches = (cache_a_k, cache_a_v, cache_b_k, cache_b_v)
    news = (new_a_k, new_a_v, new_b_k, new_b_v)
    in_specs = ([smem, smem, _resident(rel_rows.shape, lambda b: (0, 0)), q_blk(n_slabs), q_blk(n_slabs),
                 q_blk(2 * n_slabs)] + [cache_blk(t) for t in caches] + [new_blk(t) for t in news])
    kern = functools.partial(_sample_attn_kernel, tq=tq, n_slabs=n_slabs)
    return pl.pallas_call(
        kern, grid=(n_batches,), in_specs=in_specs, out_specs=q_blk(n_slabs),
        out_shape=jax.ShapeDtypeStruct(qa.shape, _BF16),
        scratch_shapes=[pltpu.VMEM((n_slabs, 2 * tq, _window(A_REACH, tq)), _F32),
                        pltpu.VMEM((n_slabs, 2 * tq, _window(B_REACH, tq)), _F32)],
        compiler_params=_params(1), name="sample_attn")(slopes, sinks, rel_rows, qa, qb, gates, *caches, *news)


def _gelu(x):
    return 0.5 * x * (1.0 + lax.erf(x * (2.0 ** -0.5)))


def _ffn_kernel(*refs, d_ff, seg, carried):
    if carried:
        x_ref, m_ref, wo_ref, n2_ref, wu_ref, cw_ref, cb_ref, wd_ref, y_ref, st_ref, carry, act = refs
    else:
        x_ref, m_ref, wo_ref, n2_ref, wu_ref, cw_ref, cb_ref, wd_ref, prev_ref, y_ref, st_ref, act = refs
    rows = x_ref.shape[1]
    n_seg = rows // seg

    if carried:
        @pl.when(pl.program_id(1) == 0)
        def _():
            carry[...] = jnp.zeros(carry.shape, carry.dtype)

    mixed = jnp.concatenate([m_ref[0, s] for s in range(m_ref.shape[1])], axis=-1)
    x1 = x_ref[0] + jnp.dot(mixed, wo_ref[...], preferred_element_type=_F32)
    h = _rmsnorm_rows(x1, n2_ref[...]).astype(_BF16)
    r = jnp.bitwise_and(lax.broadcasted_iota(jnp.int32, (rows, 1), 0), seg - 1)

    def conv(col):
        u = jnp.dot(h, wu_ref[:, col:col + FF_CHUNK], preferred_element_type=_F32)
        cs = slice(col, col + FF_CHUNK)
        if carried:
            p0, p1 = carry[SUBLANES - 2:SUBLANES - 1, cs], carry[SUBLANES - 1:SUBLANES, cs]
            carry[:, cs] = u[rows - SUBLANES:, :]
        else:
            per_seg = lambda j: jnp.concatenate(
                [jnp.broadcast_to(prev_ref[g, j:j + 1, cs], (seg, FF_CHUNK)) for g in range(n_seg)], axis=0)
            p0, p1 = per_seg(0), per_seg(1)
        for g in range(n_seg):
            st_ref[g, :, cs] = u[(g + 1) * seg - SUBLANES:(g + 1) * seg, :]
        u1 = jnp.where(r == 0, p1, pltpu.roll(u, 1, 0))
        u2 = jnp.where(r == 0, p0, jnp.where(r == 1, p1, pltpu.roll(u, 2, 0)))
        return cb_ref[:, cs] + cw_ref[0:1, cs] * u2 + cw_ref[1:2, cs] * u1 + cw_ref[2:3, cs] * u

    for c in range(0, d_ff, FF_CHUNK):
        a = conv(c)
        g = conv(d_ff + c)
        act[:, c:c + FF_CHUNK] = (_gelu(a) * g).astype(act.dtype)

    y_ref[0] = x1 + jnp.dot(act[...], wd_ref[...], preferred_element_type=_F32)


def _ffn(x, mixed, w_out, n2, w_up, conv_w, conv_b, w_down, prev=None, *, layer, tile, seg):
    bsz, seq, d_model = x.shape
    d_ff = w_down.shape[1]
    n_tiles = seq // tile
    carried = prev is None
    n_seg = 1 if carried else tile // seg
    assert seg & (seg - 1) == 0 and (seg == seq if carried else tile % seg == 0)

    def row(b, i):
        return (b, i, 0)

    def slab_row(b, i):
        return (b, 0, i, 0)

    def segs(b, i):
        return (b if carried else b * n_tiles + i, 0, 0)

    def const2(b, i):
        return (0, 0)

    operands = [x, mixed, w_out, n2, w_up, conv_w, conv_b, w_down]
    in_specs = [pl.BlockSpec((1, tile, d_model), row),
                pl.BlockSpec((1, mixed.shape[1], tile, LANES), slab_row),
                _layer_resident(w_out, layer), _resident((1, d_model), const2),
                _layer_resident(w_up, layer), _resident(conv_w.shape, const2),
                _resident((1, 2 * d_ff), const2), _layer_resident(w_down, layer)]
    scratch = [pltpu.VMEM((tile, d_ff), _BF16)]
    if carried:
        scratch.insert(0, pltpu.VMEM((SUBLANES, 2 * d_ff), _F32))
    else:
        operands.append(prev)
        in_specs.append(pl.BlockSpec((n_seg,) + prev.shape[1:], segs))
    n_state = bsz * (1 if carried else n_tiles * n_seg)
    out_specs = (pl.BlockSpec((1, tile, d_model), row), pl.BlockSpec((n_seg, SUBLANES, 2 * d_ff), segs))
    out_shape = (jax.ShapeDtypeStruct((bsz, seq, d_model), _F32),
                 jax.ShapeDtypeStruct((n_state, SUBLANES, 2 * d_ff), _F32))
    kern = functools.partial(_ffn_kernel, d_ff=d_ff, seg=tile if carried else seg, carried=carried)
    return pl.pallas_call(
        kern, grid=(bsz, n_tiles), in_specs=in_specs, out_specs=out_specs, out_shape=out_shape,
        scratch_shapes=scratch, compiler_params=_params(2), name="ffn")(*operands)


def _rel_rows(table):
    n_low = B_REACH - REL_CLIP + ATTN_SUB - 1
    n_high = REL_ROW - n_low - table.shape[0]
    rows = jnp.concatenate([jnp.broadcast_to(table[-1:], (n_low, table.shape[1])),
                            table[::-1],
                            jnp.broadcast_to(table[:1], (n_high, table.shape[1]))], axis=0)
    return rows.T


def _tile2(g, scale=1.0):
    return (jnp.tile(g, 2) * scale).reshape(1, LANES)


def kernel(x_prompt, x_sample, cache_a_k, cache_a_v, cache_b_k, cache_b_v, cache_ffn_conv, norm1_g, w_in, b_gate, qn_a_g, kn_a_g, qn_b_g, kn_b_g, sinks_a, rel_bias_b, w_out, norm2_g, w_up, conv_w, conv_b, w_down):
    depth = w_in.shape[0]
    bsz, seq, d_model = x_prompt.shape
    dbs, dseq, _ = x_sample.shape
    n_heads = d_model // HEAD_DIM
    assert cache_a_k.shape[2] == A_REACH and cache_b_k.shape[2] == B_REACH
    assert seq % PROJ_TILE == 0 and PROJ_TILE >= B_REACH and PROJ_TILE % ATTN_TILE == 0 and dseq <= CHUNK

    slopes = 2.0 ** (-8.0 * jnp.arange(1, n_heads + 1, dtype=_F32) / n_heads)
    xp, xs = x_prompt, x_sample.reshape(1, dbs * dseq, d_model)
    heads = lambda t, rows: t.reshape(-1, rows, t.shape[-1] // HEAD_DIM, HEAD_DIM)
    w_in, w_out, w_up, w_down = (_to_bf16(w) for w in (w_in, w_out, w_up, w_down))
    flat = lambda c: c.reshape(depth * dbs, c.shape[2], -1).astype(_BF16)
    caches = [flat(c) for c in (cache_a_k, cache_a_v, cache_b_k, cache_b_v)]
    pk, sk = [], []
    for l in range(depth):
        n1, n2 = norm1_g[l].reshape(1, -1), norm2_g[l].reshape(1, -1)
        bg, cb = b_gate[l].reshape(1, -1), conv_b[l].reshape(1, -1)
        gains = (_tile2(qn_a_g[l], Q_SCALE), _tile2(kn_a_g[l]), _tile2(qn_b_g[l], Q_SCALE), _tile2(kn_b_g[l]))
        rel = _rel_rows(rel_bias_b[l])

        qa, qb, g, ka, va, kb, vb, kat, vat, kbt, vbt = _proj(
            xp, n1, w_in, bg, *gains, layer=l, tile=PROJ_TILE)
        mixed = _attention(slopes, sinks_a[l], rel, qa, qb, g, ka, va, kb, vb)
        xp, st = _ffn(xp, mixed, w_out, n2, w_up, conv_w[l], cb, w_down, layer=l, tile=PROJ_TILE, seg=seq)
        pk.append((heads(kat[:, -A_REACH:], A_REACH), heads(vat[:, -A_REACH:], A_REACH),
                   heads(kbt, B_REACH), heads(vbt, B_REACH), st[:, -(CONV_W - 1):]))

        qa, qb, g, kat, vat, kbt, vbt = _proj(
            xs, n1, w_in, bg, *gains, layer=l, tile=dbs * dseq, emit_kv=False)
        mixed = _sample_attention(slopes, sinks_a[l], rel, qa, qb, g, *caches, kat, vat, kbt, vbt, layer=l, tq=dseq)
        xs, st = _ffn(xs, mixed, w_out, n2, w_up, conv_w[l], cb, w_down, cache_ffn_conv[l],
                      layer=l, tile=dbs * dseq, seg=dseq)
        sk.append((heads(kat, dseq), heads(vat, dseq), heads(kbt, dseq), heads(vbt, dseq),
                   st[:, -(CONV_W - 1):]))

    stk = lambda states, i: jnp.stack([s[i] for s in states])
    return (xp, xs.reshape(dbs, dseq, d_model),
            stk(pk, 0), stk(pk, 1), stk(pk, 2), stk(pk, 3), stk(pk, 4),
            stk(sk, 0), stk(sk, 1), stk(sk, 2), stk(sk, 3), stk(sk, 4))
```

```python
import functools

import jax
import jax.numpy as jnp
from jax import lax
from jax.experimental import pallas as pl
from jax.experimental.pallas import tpu as pltpu

HEAD_DIM = 64
CHUNK = 64
A_GROUP = 4
A_PREV = 2
B_PREV = 8
A_REACH = A_PREV * CHUNK
B_REACH = B_PREV * CHUNK
REL_CLIP = 128
CONV_W = 3
EPS = 1e-6
NEG_INF = -1e30
LOG2E = 1.4426950408889634
Q_SCALE = HEAD_DIM ** -0.5 * LOG2E

LANES = 128
SUBLANES = 8
MXU_WIDTH = 256
VMEM_LIMIT_BYTES = 56 * 1024 * 1024

PROJ_TILE = 512
ATTN_TILE = 256
ATTN_SUB = 128
FF_CHUNK = 256
REL_ROW = 1024

_BF16 = jnp.bfloat16
_F32 = jnp.float32


def _resident(shape, index_map):
    return pl.BlockSpec(shape, index_map, pipeline_mode=pl.Buffered(1))


def _layer_resident(w, layer):
    return pl.BlockSpec((None,) + w.shape[1:], lambda *_: (layer, 0, 0), pipeline_mode=pl.Buffered(1))


def _params(n_axes, flags=None):
    return pltpu.CompilerParams(dimension_semantics=("arbitrary",) * n_axes,
                                vmem_limit_bytes=VMEM_LIMIT_BYTES, flags=flags)


def _rmsnorm_rows(x, g):
    return x * lax.rsqrt(jnp.mean(x * x, axis=-1, keepdims=True) + EPS) * g


def _lane_is_low():
    return lax.broadcasted_iota(jnp.int32, (1, LANES), 1) < HEAD_DIM


def _headnorm_slab(z, g2):
    low = _lane_is_low()
    sq = z * z
    s_lo = jnp.sum(jnp.where(low, sq, 0.0), axis=-1, keepdims=True)
    s_hi = jnp.sum(jnp.where(low, 0.0, sq), axis=-1, keepdims=True)
    ms = jnp.where(low, s_lo, s_hi) * (1.0 / HEAD_DIM)
    return z * lax.rsqrt(ms + EPS) * g2


CAST_BLOCK_BYTES = 6 * 1024 * 1024


def _cast_kernel(x_ref, o_ref):
    o_ref[...] = x_ref[...].astype(o_ref.dtype)


def _to_bf16(w):
    depth, rows, cols = w.shape
    packing = 2 * SUBLANES
    fits = [r for r in range(packing, rows + 1, packing) if rows % r == 0 and r * cols * 4 <= CAST_BLOCK_BYTES]
    rb = max(fits)
    spec = pl.BlockSpec((1, rb, cols), lambda l, i: (l, i, 0))
    return pl.pallas_call(_cast_kernel, grid=(depth, rows // rb), in_specs=[spec], out_specs=spec,
                          out_shape=jax.ShapeDtypeStruct(w.shape, _BF16), compiler_params=_params(2),
                          name="cast")(w)


def _proj_kernel(x_ref, n1_ref, w_ref, bg_ref, gqa_ref, gka_ref, gqb_ref, gkb_ref, qa_ref, qb_ref, g_ref, *outs,
                 d_model, a_kv, emit_kv):
    ka_ref, va_ref, kb_ref, vb_ref = outs[:4] if emit_kv else (None,) * 4
    kat_ref, vat_ref, kbt_ref, vbt_ref = outs[-4:]
    o_ka = d_model
    o_va = o_ka + a_kv
    o_qb = o_va + a_kv
    o_kb = o_qb + d_model
    o_vb = o_kb + d_model
    o_g = o_vb + d_model

    h = _rmsnorm_rows(x_ref[0], n1_ref[...]).astype(_BF16)
    low = _lane_is_low()

    def slabs(base, width):
        for c in range(0, width, MXU_WIDTH):
            z = jnp.dot(h, w_ref[:, base + c:base + c + MXU_WIDTH], preferred_element_type=_F32)
            for t in range(MXU_WIDTH // LANES):
                yield c // LANES + t, z[:, t * LANES:(t + 1) * LANES]

    def emit(out_ref, base, width, gain_ref=None, tail_ref=None, twice=False, feature_major=False):
        def put(idx, y):
            if feature_major:
                yt = y.T.astype(out_ref.dtype)
                for t in range(out_ref.shape[1]):
                    out_ref[0, t, idx] = yt[:, t * ATTN_TILE:(t + 1) * ATTN_TILE]
            else:
                out_ref[0, idx] = y.astype(out_ref.dtype)

        for s, y in slabs(base, width):
            if gain_ref is not None:
                y = _headnorm_slab(y, gain_ref[...])
            if tail_ref is not None:
                tail_ref[0, :, s * LANES:(s + 1) * LANES] = y
            if out_ref is None:
                continue
            if twice:
                y_sw = pltpu.roll(y, HEAD_DIM, 1)
                put(2 * s, jnp.where(low, y, y_sw))
                put(2 * s + 1, jnp.where(low, y_sw, y))
            else:
                put(s, y)

    emit(qa_ref, 0, d_model, gqa_ref)
    emit(ka_ref, o_ka, a_kv, gka_ref, kat_ref, twice=True)
    emit(va_ref, o_va, a_kv, None, vat_ref, twice=True, feature_major=True)
    emit(qb_ref, o_qb, d_model, gqb_ref)
    emit(kb_ref, o_kb, d_model, gkb_ref, kbt_ref)
    emit(vb_ref, o_vb, d_model, None, vbt_ref, feature_major=True)
    for s, y in slabs(o_g, 2 * d_model):
        c = s * LANES
        g_ref[0, s] = jax.nn.sigmoid(y + bg_ref[:, c:c + LANES]).astype(g_ref.dtype)


def _proj(x, n1, w_in, b_gate, gqa, gka, gqb, gkb, *, layer, tile, emit_kv=True):
    bsz, seq, d_model = x.shape
    a_kv = d_model // A_GROUP
    ns_d, ns_kv = d_model // LANES, a_kv // LANES

    def row(b, j):
        return (b, j, 0)

    def slab_row(b, j):
        return (b, 0, j, 0)

    def key_blocks(b, j):
        return (b, j, 0, 0, 0)

    def const2(b, j):
        return (0, 0)

    def tail(b, j):
        return (b, 0, 0)

    bf = lambda ns: jax.ShapeDtypeStruct((bsz, ns, seq, LANES), _BF16)
    bf_t = lambda ns: jax.ShapeDtypeStruct((bsz, seq // ATTN_TILE, ns, LANES, ATTN_TILE), _BF16)
    f32 = lambda ns: jax.ShapeDtypeStruct((bsz, tile, ns * LANES), _F32)
    blk = lambda ns: pl.BlockSpec((1, ns, tile, LANES), slab_row)
    blk_t = lambda ns: pl.BlockSpec((1, tile // ATTN_TILE, ns, LANES, ATTN_TILE), key_blocks)
    blk_tail = lambda ns: pl.BlockSpec((1, tile, ns * LANES), tail)
    out_shape = [bf(ns_d), bf(ns_d), bf(2 * ns_d)]
    out_specs = [blk(ns_d), blk(ns_d), blk(2 * ns_d)]
    if emit_kv:
        out_shape += [bf(2 * ns_kv), bf_t(2 * ns_kv), bf(ns_d), bf_t(ns_d)]
        out_specs += [blk(2 * ns_kv), blk_t(2 * ns_kv), blk(ns_d), blk_t(ns_d)]
    out_shape += [f32(ns_kv), f32(ns_kv), f32(ns_d), f32(ns_d)]
    out_specs += [blk_tail(ns_kv), blk_tail(ns_kv), blk_tail(ns_d), blk_tail(ns_d)]
    in_specs = [pl.BlockSpec((1, tile, d_model), row),
                _resident((1, d_model), const2),
                _layer_resident(w_in, layer),
                _resident((1, 2 * d_model), const2),
                _resident((1, LANES), const2), _resident((1, LANES), const2),
                _resident((1, LANES), const2), _resident((1, LANES), const2)]
    kern = functools.partial(_proj_kernel, d_model=d_model, a_kv=a_kv, emit_kv=emit_kv)
    return pl.pallas_call(kern, grid=(bsz, seq // tile), in_specs=in_specs, out_specs=tuple(out_specs),
                          out_shape=tuple(out_shape), compiler_params=_params(2), name="proj")(
                              x, n1, w_in, b_gate, gqa, gka, gqb, gkb)


def _band_mask(tq, w, n_prev):
    shift = CHUNK.bit_length() - 1
    qc = lax.shift_right_logical(lax.broadcasted_iota(jnp.int32, (tq, w), 0), shift)
    kc = lax.shift_right_logical(lax.broadcasted_iota(jnp.int32, (tq, w), 1), shift)
    d = kc - qc
    return (d >= 0) & (d <= n_prev)


def _window(reach, tq):
    return -(-(reach + tq) // LANES) * LANES


def _stack_heads(qs):
    low = _lane_is_low()
    zero = jnp.zeros_like(qs)
    return jnp.concatenate([jnp.where(low, qs, zero), jnp.where(low, zero, qs)], axis=0)


def _with_ones(v):
    return jnp.concatenate([v, jnp.ones_like(v)], axis=1)


def _normalised(o2, extra_den, rows):
    den = o2[:, LANES:]
    if extra_den is not None:
        den = den + extra_den
    o = o2[:, :LANES] / den
    return jnp.where(_lane_is_low(), o[:rows], o[rows:])


def _init_bias(slopes_ref, rel_ref, bias_a, bias_b, ts, keys_major=False):
    n_slabs = bias_a.shape[0]
    wa, wb = bias_a.shape[1 if keys_major else 2], bias_b.shape[1 if keys_major else 2]
    qpos = lax.broadcasted_iota(jnp.int32, (ts, wa), 0)
    kpos = lax.broadcasted_iota(jnp.int32, (ts, wa), 1) - A_REACH
    dist = jnp.abs(qpos - kpos).astype(_F32)
    band_a = _band_mask(ts, wa, A_PREV)
    band_b = _band_mask(ts, wb, B_PREV)
    for h in range(2 * n_slabs):
        of_head = slice((h % 2) * ts, (h % 2 + 1) * ts)
        ba = jnp.where(band_a, (-LOG2E * slopes_ref[h]) * dist, NEG_INF)
        rows = jnp.broadcast_to(rel_ref[h:h + 1, :], (ts, REL_ROW))
        rows = pltpu.roll(rows, REL_ROW - ATTN_SUB + 1, 1, stride=1, stride_axis=0)
        bb = jnp.where(band_b, LOG2E * rows[:, :wb], NEG_INF)
        if keys_major:
            bias_a[h // 2, :, of_head] = ba.T
            bias_b[h // 2, :, of_head] = bb.T
        else:
            bias_a[h // 2, of_head, :] = ba
            bias_b[h // 2, of_head, :] = bb


def _attn_kernel(*refs, tq, ts, n_blocks, n_slabs):
    na, nb = n_blocks
    slopes_ref, sinks_ref, rel_ref, qa_ref, qb_ref, g_ref = refs[:6]
    kv = list(refs[6:6 + 2 * (na + nb)])
    ka_refs, va_refs, kb_refs, vb_refs = kv[:na], kv[na:2 * na], kv[2 * na:2 * na + nb], kv[2 * na + nb:]
    o_ref, bias_a, bias_b = refs[6 + len(kv):9 + len(kv)]
    bufs = refs[9 + len(kv):]
    sa_ref, sb_ref, pa_ref, pb_ref, da_ref = (bufs[2 * n:2 * n + 2] for n in range(5))
    wa, wb = bias_a.shape[1], bias_b.shape[1]
    b = pl.program_id(0)
    i = pl.program_id(1)

    @pl.when((b == 0) & (i == 0))
    def _():
        _init_bias(slopes_ref, rel_ref, bias_a, bias_b, ts, keys_major=True)

    feat_is_lo = lax.broadcasted_iota(jnp.int32, (LANES, ts), 0) < HEAD_DIM
    col_is_lo = lax.broadcasted_iota(jnp.int32, (1, 2 * ts), 1) < ts

    def window(blocks, slab, u, reach, feature_major):
        if feature_major:
            cut = lambda r, lo, hi: r[0, 0, slab, :, lo:hi]
        else:
            cut = lambda r, lo, hi: r[0, slab, lo:hi, :]
        lo, hi = u * ts - reach, (u + 1) * ts
        pieces = []
        for j, r in enumerate(blocks):
            base = (j - len(blocks) + 1) * tq
            if max(lo, base) < min(hi, base + tq):
                pieces.append(cut(r, max(lo, base) - base, min(hi, base + tq) - base))
        return pieces[0] if len(pieces) == 1 else jnp.concatenate(pieces, axis=1 if feature_major else 0)

    def heads_side_by_side(qs):
        qt = qs.astype(_F32).T.astype(_BF16)
        zero = jnp.zeros_like(qt)
        return jnp.concatenate([jnp.where(feat_is_lo, qt, zero), jnp.where(feat_is_lo, zero, qt)], axis=1)

    def softmax_weights(s, sink_row):
        m = jnp.max(s, axis=0, keepdims=True)
        if sink_row is not None:
            m = jnp.maximum(m, sink_row)
        p = jnp.exp2(s - m).astype(_BF16)
        return p, None if sink_row is None else jnp.broadcast_to(jnp.exp2(sink_row - m), (SUBLANES, 2 * ts))

    def slab_out(vt, p, extra_den):
        ones = jnp.ones((2 * SUBLANES, vt.shape[1]), vt.dtype)
        o2 = jnp.dot(jnp.concatenate([vt, ones], axis=0), p, preferred_element_type=_F32)
        den = o2[LANES:LANES + SUBLANES]
        if extra_den is not None:
            den = den + extra_den
        o = o2[:LANES] / den[0:1]
        return jnp.concatenate([o[:HEAD_DIM, :ts], o[HEAD_DIM:, ts:]], axis=0).T

    def run(masked):
        valid = []
        for u in range(tq // ts):
            if masked:
                start = i * tq + u * ts
                key = lambda w: lax.broadcasted_iota(jnp.int32, (w, 2 * ts), 0)
                valid.append((jnp.where(key(wa) >= A_REACH - start, 0.0, NEG_INF),
                              jnp.where(key(wb) >= B_REACH - start, 0.0, NEG_INF)))
            else:
                valid.append((None, None))

        def scores(k, u, par):
            rows_u = slice(u * ts, (u + 1) * ts)
            sa_ref[par][...] = jnp.dot(window(ka_refs, k // 2, u, A_REACH, False),
                                       heads_side_by_side(qa_ref[0, k, rows_u]), preferred_element_type=_F32)
            sb_ref[par][...] = jnp.dot(window(kb_refs, k, u, B_REACH, False),
                                       heads_side_by_side(qb_ref[0, k, rows_u]), preferred_element_type=_F32)

        def softmax(k, u, par):
            valid_a, valid_b = valid[u]
            sink_row = LOG2E * jnp.where(col_is_lo, sinks_ref[2 * k], sinks_ref[2 * k + 1])
            s = sa_ref[par][...] + bias_a[k]
            pa_ref[par][...], da_ref[par][...] = softmax_weights(s if valid_a is None else s + valid_a, sink_row)
            s = sb_ref[par][...] + bias_b[k]
            pb_ref[par][...], _ = softmax_weights(s if valid_b is None else s + valid_b, None)

        def values(k, u, par):
            rows_u = slice(u * ts, (u + 1) * ts)
            oa = slab_out(window(va_refs, k // 2, u, A_REACH, True), pa_ref[par][...], da_ref[par][...])
            ob = slab_out(window(vb_refs, k, u, B_REACH, True), pb_ref[par][...], None)
            mixed = g_ref[0, k, rows_u].astype(_F32) * oa + g_ref[0, n_slabs + k, rows_u].astype(_F32) * ob
            o_ref[0, k, rows_u] = mixed.astype(o_ref.dtype)

        scores(0, 0, 0)
        softmax(0, 0, 0)
        scores(0, 1, 1)

        def body(k, carry):
            values(k - 1, 0, 0)
            softmax(k - 1, 1, 1)
            scores(k, 0, 0)
            values(k - 1, 1, 1)
            softmax(k, 0, 0)
            scores(k, 1, 1)
            return carry

        lax.fori_loop(1, n_slabs, body, 0)
        values(n_slabs - 1, 0, 0)
        softmax(n_slabs - 1, 1, 1)
        values(n_slabs - 1, 1, 1)

    n_early = B_REACH // tq
    pl.when(i < n_early)(lambda: run(True))
    pl.when(i >= n_early)(lambda: run(False))


def _attention(slopes, sinks, rel_rows, qa, qb, gates, ka, va, kb, vb):
    bsz, n_slabs, seq, _ = qa.shape
    tq, ts = ATTN_TILE, ATTN_SUB
    assert tq == 2 * ts and seq % tq == 0 and B_REACH % tq == 0
    n_blocks = (1 + -(-A_REACH // tq), 1 + B_REACH // tq)

    def row(b, i):
        return (b, 0, i, 0)

    def const2(b, i):
        return (0, 0)

    def kv_specs(t, n, feature_major):
        if feature_major:
            return [pl.BlockSpec((1, 1) + t.shape[2:], lambda b, i, d=d: (b, jnp.maximum(i - d, 0), 0, 0, 0))
                    for d in range(n - 1, -1, -1)]
        return [pl.BlockSpec((1, t.shape[1], tq, LANES), lambda b, i, d=d: (b, 0, jnp.maximum(i - d, 0), 0))
                for d in range(n - 1, -1, -1)]

    smem = pl.BlockSpec(memory_space=pltpu.SMEM)
    rows = lambda ns: pl.BlockSpec((1, ns, tq, LANES), row)
    na, nb = n_blocks
    wa, wb = A_REACH + ts, B_REACH + ts
    in_specs = ([smem, smem, _resident(rel_rows.shape, const2), rows(n_slabs), rows(n_slabs), rows(2 * n_slabs)]
                + kv_specs(ka, na, False) + kv_specs(va, na, True) + kv_specs(kb, nb, False) + kv_specs(vb, nb, True))
    operands = [slopes, sinks, rel_rows, qa, qb, gates] + [ka] * na + [va] * na + [kb] * nb + [vb] * nb
    kern = functools.partial(_attn_kernel, tq=tq, ts=ts, n_blocks=n_blocks, n_slabs=n_slabs)
    return pl.pallas_call(
        kern, grid=(bsz, seq // tq), in_specs=in_specs, out_specs=rows(n_slabs),
        out_shape=jax.ShapeDtypeStruct(qa.shape, _BF16),
        scratch_shapes=[pltpu.VMEM((n_slabs, wa, 2 * ts), _F32), pltpu.VMEM((n_slabs, wb, 2 * ts), _F32)]
        + [pltpu.VMEM(shape, dt)
           for shape, dt in (((wa, 2 * ts), _F32), ((wb, 2 * ts), _F32), ((wa, 2 * ts), _BF16),
                             ((wb, 2 * ts), _BF16), ((SUBLANES, 2 * ts), _F32)) for _ in range(2)],
        compiler_params=_params(2), name="attn")(*operands)


def _sample_attn_kernel(slopes_ref, sinks_ref, rel_ref, qa_ref, qb_ref, g_ref,
                        cak_ref, cav_ref, cbk_ref, cbv_ref, nak_ref, nav_ref, nbk_ref, nbv_ref,
                        o_ref, bias_a, bias_b, *, tq, n_slabs):
    @pl.when(pl.program_id(0) == 0)
    def _():
        _init_bias(slopes_ref, rel_ref, bias_a, bias_b, tq)

    low = _lane_is_low()
    row_is_lo = lax.broadcasted_iota(jnp.int32, (2 * tq, 1), 0) < tq

    def both_halves(x, half):
        sw = pltpu.roll(x, HEAD_DIM, 1)
        return (jnp.where(low, x, sw) if half == 0 else jnp.where(low, sw, x)).astype(_BF16)

    def attend(qs, parts, bias, sink_col):
        qq = _stack_heads(qs)
        ss = [lax.dot_general(qq, k, (((1,), (1,)), ((), ())), preferred_element_type=_F32)
              + bias[:, c0:c0 + k.shape[0]] for k, _, c0 in parts]
        m = functools.reduce(jnp.maximum, [jnp.max(s, axis=-1, keepdims=True) for s in ss])
        if sink_col is not None:
            m = jnp.maximum(m, sink_col)
        o2 = sum(jnp.dot(jnp.exp2(s - m).astype(_BF16), _with_ones(v), preferred_element_type=_F32)
                 for s, (_, v, _) in zip(ss, parts))
        return _normalised(o2, None if sink_col is None else jnp.exp2(sink_col - m), tq)

    for s in range(n_slabs):
        kvh = s // 2
        la = slice(kvh // 2 * LANES, (kvh // 2 + 1) * LANES)
        dup = lambda r: both_halves(r[0, :, la], kvh % 2)
        sink_col = LOG2E * jnp.where(row_is_lo, sinks_ref[2 * s], sinks_ref[2 * s + 1])
        oa = attend(qa_ref[0, s], [(dup(cak_ref), dup(cav_ref), 0), (dup(nak_ref), dup(nav_ref), A_REACH)],
                    bias_a[s], sink_col)
        lb = slice(s * LANES, (s + 1) * LANES)
        cut = lambda r: r[0, :, lb].astype(_BF16)
        ob = attend(qb_ref[0, s], [(cut(cbk_ref), cut(cbv_ref), 0), (cut(nbk_ref), cut(nbv_ref), B_REACH)],
                    bias_b[s], None)
        mixed = g_ref[0, s].astype(_F32) * oa + g_ref[0, n_slabs + s].astype(_F32) * ob
        o_ref[0, s] = mixed.astype(o_ref.dtype)


def _sample_attention(slopes, sinks, rel_rows, qa, qb, gates, cache_a_k, cache_a_v, cache_b_k, cache_b_v,
                      new_a_k, new_a_v, new_b_k, new_b_v, *, layer, tq):
    _, n_slabs, rows, _ = qa.shape
    n_batches = rows // tq

    def q_rows(b):
        return (0, 0, b, 0)

    def cached(b):
        return (layer * n_batches + b, 0, 0)

    def new_rows(b):
        return (0, b, 0)

    smem = pl.BlockSpec(memory_space=pltpu.SMEM)
    q_blk = lambda ns: pl.BlockSpec((1, ns, tq, LANES), q_rows)
    cache_blk = lambda t: pl.BlockSpec((1,) + t.shape[1:], cached)
    new_blk = lambda t: pl.BlockSpec((1, tq, t.shape[2]), new_rows)
    caches = (cache_a_k, cache_a_v, cache_b_k, cache_b_v)
    news = (new_a_k, new_a_v, new_b_k, new_b_v)
    in_specs = ([smem, smem, _resident(rel_rows.shape, lambda b: (0, 0)), q_blk(n_slabs), q_blk(n_slabs),
                 q_blk(2 * n_slabs)] + [cache_blk(t) for t in caches] + [new_blk(t) for t in news])
    kern = functools.partial(_sample_attn_kernel, tq=tq, n_slabs=n_slabs)
    return pl.pallas_call(
        kern, grid=(n_batches,), in_specs=in_specs, out_specs=q_blk(n_slabs),
        out_shape=jax.ShapeDtypeStruct(qa.shape, _BF16),
        scratch_shapes=[pltpu.VMEM((n_slabs, 2 * tq, _window(A_REACH, tq)), _F32),
                        pltpu.VMEM((n_slabs, 2 * tq, _window(B_REACH, tq)), _F32)],
        compiler_params=_params(1), name="sample_attn")(slopes, sinks, rel_rows, qa, qb, gates, *caches, *news)


def _gelu(x):
    return 0.5 * x * (1.0 + lax.erf(x * (2.0 ** -0.5)))


def _ffn_kernel(*refs, d_ff, seg, carried):
    if carried:
        x_ref, m_ref, wo_ref, n2_ref, wu_ref, cw_ref, cb_ref, wd_ref, y_ref, st_ref, carry, act = refs
    else:
        x_ref, m_ref, wo_ref, n2_ref, wu_ref, cw_ref, cb_ref, wd_ref, prev_ref, y_ref, st_ref, act = refs
    rows = x_ref.shape[1]
    n_seg = rows // seg

    if carried:
        @pl.when(pl.program_id(1) == 0)
        def _():
            carry[...] = jnp.zeros(carry.shape, carry.dtype)

    mixed = jnp.concatenate([m_ref[0, s] for s in range(m_ref.shape[1])], axis=-1)
    x1 = x_ref[0] + jnp.dot(mixed, wo_ref[...], preferred_element_type=_F32)
    h = _rmsnorm_rows(x1, n2_ref[...]).astype(_BF16)
    r = jnp.bitwise_and(lax.broadcasted_iota(jnp.int32, (rows, 1), 0), seg - 1)

    def conv(col):
        u = jnp.dot(h, wu_ref[:, col:col + FF_CHUNK], preferred_element_type=_F32)
        cs = slice(col, col + FF_CHUNK)
        if carried:
            p0, p1 = carry[SUBLANES - 2:SUBLANES - 1, cs], carry[SUBLANES - 1:SUBLANES, cs]
            carry[:, cs] = u[rows - SUBLANES:, :]
        else:
            per_seg = lambda j: jnp.concatenate(
                [jnp.broadcast_to(prev_ref[g, j:j + 1, cs], (seg, FF_CHUNK)) for g in range(n_seg)], axis=0)
            p0, p1 = per_seg(0), per_seg(1)
        for g in range(n_seg):
            st_ref[g, :, cs] = u[(g + 1) * seg - SUBLANES:(g + 1) * seg, :]
        u1 = jnp.where(r == 0, p1, pltpu.roll(u, 1, 0))
        u2 = jnp.where(r == 0, p0, jnp.where(r == 1, p1, pltpu.roll(u, 2, 0)))
        return cb_ref[:, cs] + cw_ref[0:1, cs] * u2 + cw_ref[1:2, cs] * u1 + cw_ref[2:3, cs] * u

    for c in range(0, d_ff, FF_CHUNK):
        a = conv(c)
        g = conv(d_ff + c)
        act[:, c:c + FF_CHUNK] = (_gelu(a) * g).astype(act.dtype)

    y_ref[0] = x1 + jnp.dot(act[...], wd_ref[...], preferred_element_type=_F32)


def _ffn(x, mixed, w_out, n2, w_up, conv_w, conv_b, w_down, prev=None, *, layer, tile, seg):
    bsz, seq, d_model = x.shape
    d_ff = w_down.shape[1]
    n_tiles = seq // tile
    carried = prev is None
    n_seg = 1 if carried else tile // seg
    assert seg & (seg - 1) == 0 and (seg == seq if carried else tile % seg == 0)

    def row(b, i):
        return (b, i, 0)

    def slab_row(b, i):
        return (b, 0, i, 0)

    def segs(b, i):
        return (b if carried else b * n_tiles + i, 0, 0)

    def const2(b, i):
        return (0, 0)

    operands = [x, mixed, w_out, n2, w_up, conv_w, conv_b, w_down]
    in_specs = [pl.BlockSpec((1, tile, d_model), row),
                pl.BlockSpec((1, mixed.shape[1], tile, LANES), slab_row),
                _layer_resident(w_out, layer), _resident((1, d_model), const2),
                _layer_resident(w_up, layer), _resident(conv_w.shape, const2),
                _resident((1, 2 * d_ff), const2), _layer_resident(w_down, layer)]
    scratch = [pltpu.VMEM((tile, d_ff), _BF16)]
    if carried:
        scratch.insert(0, pltpu.VMEM((SUBLANES, 2 * d_ff), _F32))
    else:
        operands.append(prev)
        in_specs.append(pl.BlockSpec((n_seg,) + prev.shape[1:], segs))
    n_state = bsz * (1 if carried else n_tiles * n_seg)
    out_specs = (pl.BlockSpec((1, tile, d_model), row), pl.BlockSpec((n_seg, SUBLANES, 2 * d_ff), segs))
    out_shape = (jax.ShapeDtypeStruct((bsz, seq, d_model), _F32),
                 jax.ShapeDtypeStruct((n_state, SUBLANES, 2 * d_ff), _F32))
    kern = functools.partial(_ffn_kernel, d_ff=d_ff, seg=tile if carried else seg, carried=carried)
    return pl.pallas_call(
        kern, grid=(bsz, n_tiles), in_specs=in_specs, out_specs=out_specs, out_shape=out_shape,
        scratch_shapes=scratch, compiler_params=_params(2), name="ffn")(*operands)


def _rel_rows(table):
    n_low = B_REACH - REL_CLIP + ATTN_SUB - 1
    n_high = REL_ROW - n_low - table.shape[0]
    rows = jnp.concatenate([jnp.broadcast_to(table[-1:], (n_low, table.shape[1])),
                            table[::-1],
                            jnp.broadcast_to(table[:1], (n_high, table.shape[1]))], axis=0)
    return rows.T


def _tile2(g, scale=1.0):
    return (jnp.tile(g, 2) * scale).reshape(1, LANES)


def kernel(x_prompt, x_sample, cache_a_k, cache_a_v, cache_b_k, cache_b_v, cache_ffn_conv, norm1_g, w_in, b_gate, qn_a_g, kn_a_g, qn_b_g, kn_b_g, sinks_a, rel_bias_b, w_out, norm2_g, w_up, conv_w, conv_b, w_down):
    depth = w_in.shape[0]
    bsz, seq, d_model = x_prompt.shape
    dbs, dseq, _ = x_sample.shape
    n_heads = d_model // HEAD_DIM
    assert cache_a_k.shape[2] == A_REACH and cache_b_k.shape[2] == B_REACH
    assert seq % PROJ_TILE == 0 and PROJ_TILE >= B_REACH and PROJ_TILE % ATTN_TILE == 0 and dseq <= CHUNK

    slopes = 2.0 ** (-8.0 * jnp.arange(1, n_heads + 1, dtype=_F32) / n_heads)
    xp, xs = x_prompt, x_sample.reshape(1, dbs * dseq, d_model)
    heads = lambda t, rows: t.reshape(-1, rows, t.shape[-1] // HEAD_DIM, HEAD_DIM)
    w_in, w_out, w_up, w_down = (_to_bf16(w) for w in (w_in, w_out, w_up, w_down))
    flat = lambda c: c.reshape(depth * dbs, c.shape[2], -1)
    caches = [flat(c) for c in (cache_a_k, cache_a_v, cache_b_k, cache_b_v)]
    pk, sk = [], []
    for l in range(depth):
        n1, n2 = norm1_g[l].reshape(1, -1), norm2_g[l].reshape(1, -1)
        bg, cb = b_gate[l].reshape(1, -1), conv_b[l].reshape(1, -1)
        gains = (_tile2(qn_a_g[l], Q_SCALE), _tile2(kn_a_g[l]), _tile2(qn_b_g[l], Q_SCALE), _tile2(kn_b_g[l]))
        rel = _rel_rows(rel_bias_b[l])

        qa, qb, g, ka, va, kb, vb, kat, vat, kbt, vbt = _proj(
            xp, n1, w_in, bg, *gains, layer=l, tile=PROJ_TILE)
        mixed = _attention(slopes, sinks_a[l], rel, qa, qb, g, ka, va, kb, vb)
        xp, st = _ffn(xp, mixed, w_out, n2, w_up, conv_w[l], cb, w_down, layer=l, tile=PROJ_TILE, seg=seq)
        pk.append((heads(kat[:, -A_REACH:], A_REACH), heads(vat[:, -A_REACH:], A_REACH),
                   heads(kbt, B_REACH), heads(vbt, B_REACH), st[:, -(CONV_W - 1):]))

        qa, qb, g, kat, vat, kbt, vbt = _proj(
            xs, n1, w_in, bg, *gains, layer=l, tile=dbs * dseq, emit_kv=False)
        mixed = _sample_attention(slopes, sinks_a[l], rel, qa, qb, g, *caches, kat, vat, kbt, vbt, layer=l, tq=dseq)
        xs, st = _ffn(xs, mixed, w_out, n2, w_up, conv_w[l], cb, w_down, cache_ffn_conv[l],
                      layer=l, tile=dbs * dseq, seg=dseq)
        sk.append((heads(kat, dseq), heads(vat, dseq), heads(kbt, dseq), heads(vbt, dseq),
                   st[:, -(CONV_W - 1):]))

    stk = lambda states, i: jnp.stack([s[i] for s in states])
    return (xp, xs.reshape(dbs, dseq, d_model),
            stk(pk, 0), stk(pk, 1), stk(pk, 2), stk(pk, 3), stk(pk, 4),
            stk(sk, 0), stk(sk, 1), stk(sk, 2), stk(sk, 3), stk(sk, 4))
```

```python
import functools

import jax
import jax.numpy as jnp
from jax import lax
from jax.experimental import pallas as pl
from jax.experimental.pallas import tpu as pltpu

HEAD_DIM = 64
CHUNK = 64
A_GROUP = 4
A_PREV = 2
B_PREV = 8
A_REACH = A_PREV * CHUNK
B_REACH = B_PREV * CHUNK
REL_CLIP = 128
CONV_W = 3
EPS = 1e-6
NEG_INF = -1e30
LOG2E = 1.4426950408889634
Q_SCALE = HEAD_DIM ** -0.5 * LOG2E

LANES = 128
SUBLANES = 8
MXU_WIDTH = 256
VMEM_LIMIT_BYTES = 56 * 1024 * 1024

PROJ_TILE = 512
ATTN_TILE = 512
ATTN_SUB = 128
FF_CHUNK = 256
REL_ROW = 1024

_BF16 = jnp.bfloat16
_F32 = jnp.float32


def _resident(shape, index_map):
    return pl.BlockSpec(shape, index_map, pipeline_mode=pl.Buffered(1))


def _layer_resident(w, layer):
    return pl.BlockSpec((None,) + w.shape[1:], lambda *_: (layer, 0, 0), pipeline_mode=pl.Buffered(1))


def _params(n_axes, flags=None):
    return pltpu.CompilerParams(dimension_semantics=("arbitrary",) * n_axes,
                                vmem_limit_bytes=VMEM_LIMIT_BYTES, flags=flags)


def _rmsnorm_rows(x, g):
    return x * lax.rsqrt(jnp.mean(x * x, axis=-1, keepdims=True) + EPS) * g


def _lane_is_low():
    return lax.broadcasted_iota(jnp.int32, (1, LANES), 1) < HEAD_DIM


def _headnorm_slab(z, g2):
    low = _lane_is_low()
    sq = z * z
    s_lo = jnp.sum(jnp.where(low, sq, 0.0), axis=-1, keepdims=True)
    s_hi = jnp.sum(jnp.where(low, 0.0, sq), axis=-1, keepdims=True)
    ms = jnp.where(low, s_lo, s_hi) * (1.0 / HEAD_DIM)
    return z * lax.rsqrt(ms + EPS) * g2


CAST_BLOCK_BYTES = 6 * 1024 * 1024


def _cast_kernel(x_ref, o_ref):
    o_ref[...] = x_ref[...].astype(o_ref.dtype)


def _to_bf16(w):
    depth, rows, cols = w.shape
    packing = 2 * SUBLANES
    fits = [r for r in range(packing, rows + 1, packing) if rows % r == 0 and r * cols * 4 <= CAST_BLOCK_BYTES]
    rb = max(fits)
    spec = pl.BlockSpec((1, rb, cols), lambda l, i: (l, i, 0))
    return pl.pallas_call(_cast_kernel, grid=(depth, rows // rb), in_specs=[spec], out_specs=spec,
                          out_shape=jax.ShapeDtypeStruct(w.shape, _BF16), compiler_params=_params(2),
                          name="cast")(w)


def _proj_kernel(x_ref, n1_ref, w_ref, bg_ref, gqa_ref, gka_ref, gqb_ref, gkb_ref, qa_ref, qb_ref, g_ref, *outs,
                 d_model, a_kv, emit_kv):
    ka_ref, va_ref, kb_ref, vb_ref = outs[:4] if emit_kv else (None,) * 4
    kat_ref, vat_ref, kbt_ref, vbt_ref = outs[-4:]
    o_ka = d_model
    o_va = o_ka + a_kv
    o_qb = o_va + a_kv
    o_kb = o_qb + d_model
    o_vb = o_kb + d_model
    o_g = o_vb + d_model

    h = _rmsnorm_rows(x_ref[0], n1_ref[...]).astype(_BF16)
    low = _lane_is_low()

    def slabs(base, width):
        for c in range(0, width, MXU_WIDTH):
            z = jnp.dot(h, w_ref[:, base + c:base + c + MXU_WIDTH], preferred_element_type=_F32)
            for t in range(MXU_WIDTH // LANES):
                yield c // LANES + t, z[:, t * LANES:(t + 1) * LANES]

    def emit(out_ref, base, width, gain_ref=None, tail_ref=None, twice=False, feature_major=False):
        def put(idx, y):
            if feature_major:
                yt = y.T.astype(out_ref.dtype)
                for t in range(out_ref.shape[1]):
                    out_ref[0, t, idx] = yt[:, t * ATTN_TILE:(t + 1) * ATTN_TILE]
            else:
                out_ref[0, idx] = y.astype(out_ref.dtype)

        for s, y in slabs(base, width):
            if gain_ref is not None:
                y = _headnorm_slab(y, gain_ref[...])
            if tail_ref is not None:
                tail_ref[0, :, s * LANES:(s + 1) * LANES] = y
            if out_ref is None:
                continue
            if twice:
                y_sw = pltpu.roll(y, HEAD_DIM, 1)
                put(2 * s, jnp.where(low, y, y_sw))
                put(2 * s + 1, jnp.where(low, y_sw, y))
            else:
                put(s, y)

    emit(qa_ref, 0, d_model, gqa_ref)
    emit(ka_ref, o_ka, a_kv, gka_ref, kat_ref, twice=True, feature_major=True)
    emit(va_ref, o_va, a_kv, None, vat_ref, twice=True)
    emit(qb_ref, o_qb, d_model, gqb_ref)
    emit(kb_ref, o_kb, d_model, gkb_ref, kbt_ref, feature_major=True)
    emit(vb_ref, o_vb, d_model, None, vbt_ref)
    for s, y in slabs(o_g, 2 * d_model):
        c = s * LANES
        g_ref[0, s] = jax.nn.sigmoid(y + bg_ref[:, c:c + LANES]).astype(g_ref.dtype)


def _proj(x, n1, w_in, b_gate, gqa, gka, gqb, gkb, *, layer, tile, emit_kv=True):
    bsz, seq, d_model = x.shape
    a_kv = d_model // A_GROUP
    ns_d, ns_kv = d_model // LANES, a_kv // LANES

    def row(b, j):
        return (b, j, 0)

    def slab_row(b, j):
        return (b, 0, j, 0)

    def key_blocks(b, j):
        return (b, j, 0, 0, 0)

    def const2(b, j):
        return (0, 0)

    def tail(b, j):
        return (b, 0, 0)

    bf = lambda ns: jax.ShapeDtypeStruct((bsz, ns, seq, LANES), _BF16)
    bf_t = lambda ns: jax.ShapeDtypeStruct((bsz, seq // ATTN_TILE, ns, LANES, ATTN_TILE), _BF16)
    f32 = lambda ns: jax.ShapeDtypeStruct((bsz, tile, ns * LANES), _F32)
    blk = lambda ns: pl.BlockSpec((1, ns, tile, LANES), slab_row)
    blk_t = lambda ns: pl.BlockSpec((1, tile // ATTN_TILE, ns, LANES, ATTN_TILE), key_blocks)
    blk_tail = lambda ns: pl.BlockSpec((1, tile, ns * LANES), tail)
    out_shape = [bf(ns_d), bf(ns_d), bf(2 * ns_d)]
    out_specs = [blk(ns_d), blk(ns_d), blk(2 * ns_d)]
    if emit_kv:
        out_shape += [bf_t(2 * ns_kv), bf(2 * ns_kv), bf_t(ns_d), bf(ns_d)]
        out_specs += [blk_t(2 * ns_kv), blk(2 * ns_kv), blk_t(ns_d), blk(ns_d)]
    out_shape += [f32(ns_kv), f32(ns_kv), f32(ns_d), f32(ns_d)]
    out_specs += [blk_tail(ns_kv), blk_tail(ns_kv), blk_tail(ns_d), blk_tail(ns_d)]
    in_specs = [pl.BlockSpec((1, tile, d_model), row),
                _resident((1, d_model), const2),
                _layer_resident(w_in, layer),
                _resident((1, 2 * d_model), const2),
                _resident((1, LANES), const2), _resident((1, LANES), const2),
                _resident((1, LANES), const2), _resident((1, LANES), const2)]
    kern = functools.partial(_proj_kernel, d_model=d_model, a_kv=a_kv, emit_kv=emit_kv)
    return pl.pallas_call(kern, grid=(bsz, seq // tile), in_specs=in_specs, out_specs=tuple(out_specs),
                          out_shape=tuple(out_shape), compiler_params=_params(2), name="proj")(
                              x, n1, w_in, b_gate, gqa, gka, gqb, gkb)


def _band_mask(tq, w, n_prev):
    shift = CHUNK.bit_length() - 1
    qc = lax.shift_right_logical(lax.broadcasted_iota(jnp.int32, (tq, w), 0), shift)
    kc = lax.shift_right_logical(lax.broadcasted_iota(jnp.int32, (tq, w), 1), shift)
    d = kc - qc
    return (d >= 0) & (d <= n_prev)


def _window(reach, tq):
    return -(-(reach + tq) // LANES) * LANES


def _stack_heads(qs):
    low = _lane_is_low()
    zero = jnp.zeros_like(qs)
    return jnp.concatenate([jnp.where(low, qs, zero), jnp.where(low, zero, qs)], axis=0)


def _with_ones(v):
    return jnp.concatenate([v, jnp.ones_like(v)], axis=1)


def _normalised(o2, extra_den, rows):
    den = o2[:, LANES:]
    if extra_den is not None:
        den = den + extra_den
    o = o2[:, :LANES] / den
    return jnp.where(_lane_is_low(), o[:rows], o[rows:])


def _init_bias(slopes_ref, rel_ref, bias_a, bias_b, ts):
    n_slabs, _, wa = bias_a.shape
    wb = bias_b.shape[2]
    qpos = lax.broadcasted_iota(jnp.int32, (ts, wa), 0)
    kpos = lax.broadcasted_iota(jnp.int32, (ts, wa), 1) - A_REACH
    dist = jnp.abs(qpos - kpos).astype(_F32)
    band_a = _band_mask(ts, wa, A_PREV)
    band_b = _band_mask(ts, wb, B_PREV)
    for h in range(2 * n_slabs):
        rows_h = slice((h % 2) * ts, (h % 2 + 1) * ts)
        bias_a[h // 2, rows_h, :] = jnp.where(band_a, (-LOG2E * slopes_ref[h]) * dist, NEG_INF)
        rows = jnp.broadcast_to(rel_ref[h:h + 1, :], (ts, REL_ROW))
        rows = pltpu.roll(rows, REL_ROW - ATTN_SUB + 1, 1, stride=1, stride_axis=0)
        bias_b[h // 2, rows_h, :] = jnp.where(band_b, LOG2E * rows[:, :wb], NEG_INF)


def _attn_kernel(*refs, tq, ts, n_blocks, n_slabs):
    na, nb = n_blocks
    slopes_ref, sinks_ref, rel_ref, qa_ref, qb_ref, g_ref = refs[:6]
    kv = list(refs[6:6 + 2 * (na + nb)])
    ka_refs, va_refs, kb_refs, vb_refs = kv[:na], kv[na:2 * na], kv[2 * na:2 * na + nb], kv[2 * na + nb:]
    o_ref, bias_a, bias_b = refs[6 + len(kv):9 + len(kv)]
    bufs = refs[9 + len(kv):]
    sa_ref, sb_ref, pa_ref, pb_ref, da_ref = (bufs[2 * n:2 * n + 2] for n in range(5))
    wa, wb = bias_a.shape[2], bias_b.shape[2]
    b = pl.program_id(0)
    i = pl.program_id(1)

    @pl.when((b == 0) & (i == 0))
    def _():
        _init_bias(slopes_ref, rel_ref, bias_a, bias_b, ts)

    row_is_lo = lax.broadcasted_iota(jnp.int32, (2 * ts, 1), 0) < ts
    col_a = lax.broadcasted_iota(jnp.int32, (1, wa), 1)
    col_b = lax.broadcasted_iota(jnp.int32, (1, wb), 1)

    def window(blocks, slab, u, reach, feature_major):
        if feature_major:
            cut = lambda r, lo, hi: r[0, 0, slab, :, lo:hi]
        else:
            cut = lambda r, lo, hi: r[0, slab, lo:hi, :]
        lo, hi = u * ts - reach, (u + 1) * ts
        pieces = []
        for j, r in enumerate(blocks):
            base = (j - len(blocks) + 1) * tq
            if max(lo, base) < min(hi, base + tq):
                pieces.append(cut(r, max(lo, base) - base, min(hi, base + tq) - base))
        return pieces[0] if len(pieces) == 1 else jnp.concatenate(pieces, axis=1 if feature_major else 0)

    def softmax_weights(s, sink_col):
        m = jnp.max(s, axis=-1, keepdims=True)
        if sink_col is not None:
            m = jnp.maximum(m, sink_col)
        p = jnp.exp2(s - m).astype(_BF16)
        return p, None if sink_col is None else jnp.broadcast_to(jnp.exp2(sink_col - m), (2 * ts, LANES))

    def run(masked):
        valid = []
        for u in range(tq // ts):
            if masked:
                start = i * tq + u * ts
                valid.append((jnp.where(col_a >= A_REACH - start, 0.0, NEG_INF),
                              jnp.where(col_b >= B_REACH - start, 0.0, NEG_INF)))
            else:
                valid.append((None, None))

        def scores(k, u, par):
            rows_u = slice(u * ts, (u + 1) * ts)
            sa_ref[par][...] = jnp.dot(_stack_heads(qa_ref[0, k, rows_u]), window(ka_refs, k // 2, u, A_REACH, True),
                                       preferred_element_type=_F32)
            sb_ref[par][...] = jnp.dot(_stack_heads(qb_ref[0, k, rows_u]), window(kb_refs, k, u, B_REACH, True),
                                       preferred_element_type=_F32)

        def softmax(k, u, par):
            valid_a, valid_b = valid[u]
            sink_col = LOG2E * jnp.where(row_is_lo, sinks_ref[2 * k], sinks_ref[2 * k + 1])
            s = sa_ref[par][...] + bias_a[k]
            pa_ref[par][...], da_ref[par][...] = softmax_weights(s if valid_a is None else s + valid_a, sink_col)
            s = sb_ref[par][...] + bias_b[k]
            pb_ref[par][...], _ = softmax_weights(s if valid_b is None else s + valid_b, None)

        def values(k, u, par):
            rows_u = slice(u * ts, (u + 1) * ts)
            oa = jnp.dot(pa_ref[par][...], _with_ones(window(va_refs, k // 2, u, A_REACH, False)),
                         preferred_element_type=_F32)
            ob = jnp.dot(pb_ref[par][...], _with_ones(window(vb_refs, k, u, B_REACH, False)),
                         preferred_element_type=_F32)
            mixed = (g_ref[0, k, rows_u].astype(_F32) * _normalised(oa, da_ref[par][...], ts)
                     + g_ref[0, n_slabs + k, rows_u].astype(_F32) * _normalised(ob, None, ts))
            o_ref[0, k, rows_u] = mixed.astype(o_ref.dtype)

        n_sub = tq // ts

        def step(k, u, first_slab=False):
            for back, stage in ((2, values), (1, softmax)):
                ku, uu = (k, u - back) if u >= back else (k - 1, u - back + n_sub)
                if not (first_slab and u < back):
                    stage(ku, uu, uu % 2)
            scores(k, u, u % 2)

        for u in range(n_sub):
            step(0, u, first_slab=True)

        def body(k, carry):
            for u in range(n_sub):
                step(k, u)
            return carry

        lax.fori_loop(1, n_slabs, body, 0)
        values(n_slabs - 1, n_sub - 2, 0)
        softmax(n_slabs - 1, n_sub - 1, 1)
        values(n_slabs - 1, n_sub - 1, 1)

    n_early = B_REACH // tq
    pl.when(i < n_early)(lambda: run(True))
    pl.when(i >= n_early)(lambda: run(False))


def _attention(slopes, sinks, rel_rows, qa, qb, gates, ka, va, kb, vb):
    bsz, n_slabs, seq, _ = qa.shape
    tq, ts = ATTN_TILE, ATTN_SUB
    assert tq % (2 * ts) == 0 and seq % tq == 0 and B_REACH % tq == 0
    n_blocks = (1 + -(-A_REACH // tq), 1 + B_REACH // tq)

    def row(b, i):
        return (b, 0, i, 0)

    def const2(b, i):
        return (0, 0)

    def kv_specs(t, n, feature_major):
        if feature_major:
            return [pl.BlockSpec((1, 1) + t.shape[2:], lambda b, i, d=d: (b, jnp.maximum(i - d, 0), 0, 0, 0))
                    for d in range(n - 1, -1, -1)]
        return [pl.BlockSpec((1, t.shape[1], tq, LANES), lambda b, i, d=d: (b, 0, jnp.maximum(i - d, 0), 0))
                for d in range(n - 1, -1, -1)]

    smem = pl.BlockSpec(memory_space=pltpu.SMEM)
    rows = lambda ns: pl.BlockSpec((1, ns, tq, LANES), row)
    na, nb = n_blocks
    wa, wb = A_REACH + ts, B_REACH + ts
    in_specs = ([smem, smem, _resident(rel_rows.shape, const2), rows(n_slabs), rows(n_slabs), rows(2 * n_slabs)]
                + kv_specs(ka, na, True) + kv_specs(va, na, False) + kv_specs(kb, nb, True) + kv_specs(vb, nb, False))
    operands = [slopes, sinks, rel_rows, qa, qb, gates] + [ka] * na + [va] * na + [kb] * nb + [vb] * nb
    kern = functools.partial(_attn_kernel, tq=tq, ts=ts, n_blocks=n_blocks, n_slabs=n_slabs)
    return pl.pallas_call(
        kern, grid=(bsz, seq // tq), in_specs=in_specs, out_specs=rows(n_slabs),
        out_shape=jax.ShapeDtypeStruct(qa.shape, _BF16),
        scratch_shapes=[pltpu.VMEM((n_slabs, 2 * ts, wa), _F32), pltpu.VMEM((n_slabs, 2 * ts, wb), _F32)]
        + [pltpu.VMEM((2 * ts, w), dt)
           for w, dt in ((wa, _F32), (wb, _F32), (wa, _BF16), (wb, _BF16), (LANES, _F32)) for _ in range(2)],
        compiler_params=_params(2), name="attn")(*operands)


def _sample_attn_kernel(slopes_ref, sinks_ref, rel_ref, qa_ref, qb_ref, g_ref,
                        cak_ref, cav_ref, cbk_ref, cbv_ref, nak_ref, nav_ref, nbk_ref, nbv_ref,
                        o_ref, bias_a, bias_b, *, tq, n_slabs):
    @pl.when(pl.program_id(0) == 0)
    def _():
        _init_bias(slopes_ref, rel_ref, bias_a, bias_b, tq)

    low = _lane_is_low()
    row_is_lo = lax.broadcasted_iota(jnp.int32, (2 * tq, 1), 0) < tq

    def both_halves(x, half):
        sw = pltpu.roll(x, HEAD_DIM, 1)
        return (jnp.where(low, x, sw) if half == 0 else jnp.where(low, sw, x)).astype(_BF16)

    def attend(qs, parts, bias, sink_col):
        qq = _stack_heads(qs)
        ss = [lax.dot_general(qq, k, (((1,), (1,)), ((), ())), preferred_element_type=_F32)
              + bias[:, c0:c0 + k.shape[0]] for k, _, c0 in parts]
        m = functools.reduce(jnp.maximum, [jnp.max(s, axis=-1, keepdims=True) for s in ss])
        if sink_col is not None:
            m = jnp.maximum(m, sink_col)
        o2 = sum(jnp.dot(jnp.exp2(s - m).astype(_BF16), _with_ones(v), preferred_element_type=_F32)
                 for s, (_, v, _) in zip(ss, parts))
        return _normalised(o2, None if sink_col is None else jnp.exp2(sink_col - m), tq)

    for s in range(n_slabs):
        kvh = s // 2
        la = slice(kvh // 2 * LANES, (kvh // 2 + 1) * LANES)
        dup = lambda r: both_halves(r[0, :, la], kvh % 2)
        sink_col = LOG2E * jnp.where(row_is_lo, sinks_ref[2 * s], sinks_ref[2 * s + 1])
        oa = attend(qa_ref[0, s], [(dup(cak_ref), dup(cav_ref), 0), (dup(nak_ref), dup(nav_ref), A_REACH)],
                    bias_a[s], sink_col)
        lb = slice(s * LANES, (s + 1) * LANES)
        cut = lambda r: r[0, :, lb].astype(_BF16)
        ob = attend(qb_ref[0, s], [(cut(cbk_ref), cut(cbv_ref), 0), (cut(nbk_ref), cut(nbv_ref), B_REACH)],
                    bias_b[s], None)
        mixed = g_ref[0, s].astype(_F32) * oa + g_ref[0, n_slabs + s].astype(_F32) * ob
        o_ref[0, s] = mixed.astype(o_ref.dtype)


def _sample_attention(slopes, sinks, rel_rows, qa, qb, gates, cache_a_k, cache_a_v, cache_b_k, cache_b_v,
                      new_a_k, new_a_v, new_b_k, new_b_v, *, layer, tq):
    _, n_slabs, rows, _ = qa.shape
    n_batches = rows // tq

    def q_rows(b):
        return (0, 0, b, 0)

    def cached(b):
        return (layer * n_batches + b, 0, 0)

    def new_rows(b):
        return (0, b, 0)

    smem = pl.BlockSpec(memory_space=pltpu.SMEM)
    q_blk = lambda ns: pl.BlockSpec((1, ns, tq, LANES), q_rows)
    cache_blk = lambda t: pl.BlockSpec((1,) + t.shape[1:], cached)
    new_blk = lambda t: pl.BlockSpec((1, tq, t.shape[2]), new_rows)
    caches = (cache_a_k, cache_a_v, cache_b_k, cache_b_v)
    news = (new_a_k, new_a_v, new_b_k, new_b_v)
    in_specs = ([smem, smem, _resident(rel_rows.shape, lambda b: (0, 0)), q_blk(n_slabs), q_blk(n_slabs),
                 q_blk(2 * n_slabs)] + [cache_blk(t) for t in caches] + [new_blk(t) for t in news])
    kern = functools.partial(_sample_attn_kernel, tq=tq, n_slabs=n_slabs)
    return pl.pallas_call(
        kern, grid=(n_batches,), in_specs=in_specs, out_specs=q_blk(n_slabs),
        out_shape=jax.ShapeDtypeStruct(qa.shape, _BF16),
        scratch_shapes=[pltpu.VMEM((n_slabs, 2 * tq, _window(A_REACH, tq)), _F32),
                        pltpu.VMEM((n_slabs, 2 * tq, _window(B_REACH, tq)), _F32)],
        compiler_params=_params(1), name="sample_attn")(slopes, sinks, rel_rows, qa, qb, gates, *caches, *news)


def _gelu(x):
    return 0.5 * x * (1.0 + lax.erf(x * (2.0 ** -0.5)))


def _ffn_kernel(*refs, d_ff, seg, carried):
    if carried:
        x_ref, m_ref, wo_ref, n2_ref, wu_ref, cw_ref, cb_ref, wd_ref, y_ref, st_ref, carry, act = refs
    else:
        x_ref, m_ref, wo_ref, n2_ref, wu_ref, cw_ref, cb_ref, wd_ref, prev_ref, y_ref, st_ref, act = refs
    rows = x_ref.shape[1]
    n_seg = rows // seg

    if carried:
        @pl.when(pl.program_id(1) == 0)
        def _():
            carry[...] = jnp.zeros(carry.shape, carry.dtype)

    mixed = jnp.concatenate([m_ref[0, s] for s in range(m_ref.shape[1])], axis=-1)
    x1 = x_ref[0] + jnp.dot(mixed, wo_ref[...], preferred_element_type=_F32)
    h = _rmsnorm_rows(x1, n2_ref[...]).astype(_BF16)
    r = jnp.bitwise_and(lax.broadcasted_iota(jnp.int32, (rows, 1), 0), seg - 1)

    def conv(col):
        u = jnp.dot(h, wu_ref[:, col:col + FF_CHUNK], preferred_element_type=_F32)
        cs = slice(col, col + FF_CHUNK)
        if carried:
            p0, p1 = carry[SUBLANES - 2:SUBLANES - 1, cs], carry[SUBLANES - 1:SUBLANES, cs]
            carry[:, cs] = u[rows - SUBLANES:, :]
        else:
            per_seg = lambda j: jnp.concatenate(
                [jnp.broadcast_to(prev_ref[g, j:j + 1, cs], (seg, FF_CHUNK)) for g in range(n_seg)], axis=0)
            p0, p1 = per_seg(0), per_seg(1)
        for g in range(n_seg):
            st_ref[g, :, cs] = u[(g + 1) * seg - SUBLANES:(g + 1) * seg, :]
        u1 = jnp.where(r == 0, p1, pltpu.roll(u, 1, 0))
        u2 = jnp.where(r == 0, p0, jnp.where(r == 1, p1, pltpu.roll(u, 2, 0)))
        return cb_ref[:, cs] + cw_ref[0:1, cs] * u2 + cw_ref[1:2, cs] * u1 + cw_ref[2:3, cs] * u

    for c in range(0, d_ff, FF_CHUNK):
        a = conv(c)
        g = conv(d_ff + c)
        act[:, c:c + FF_CHUNK] = (_gelu(a) * g).astype(act.dtype)

    y_ref[0] = x1 + jnp.dot(act[...], wd_ref[...], preferred_element_type=_F32)


def _ffn(x, mixed, w_out, n2, w_up, conv_w, conv_b, w_down, prev=None, *, layer, tile, seg):
    bsz, seq, d_model = x.shape
    d_ff = w_down.shape[1]
    n_tiles = seq // tile
    carried = prev is None
    n_seg = 1 if carried else tile // seg
    assert seg & (seg - 1) == 0 and (seg == seq if carried else tile % seg == 0)

    def row(b, i):
        return (b, i, 0)

    def slab_row(b, i):
        return (b, 0, i, 0)

    def segs(b, i):
        return (b if carried else b * n_tiles + i, 0, 0)

    def const2(b, i):
        return (0, 0)

    operands = [x, mixed, w_out, n2, w_up, conv_w, conv_b, w_down]
    in_specs = [pl.BlockSpec((1, tile, d_model), row),
                pl.BlockSpec((1, mixed.shape[1], tile, LANES), slab_row),
                _layer_resident(w_out, layer), _resident((1, d_model), const2),
                _layer_resident(w_up, layer), _resident(conv_w.shape, const2),
                _resident((1, 2 * d_ff), const2), _layer_resident(w_down, layer)]
    scratch = [pltpu.VMEM((tile, d_ff), _BF16)]
    if carried:
        scratch.insert(0, pltpu.VMEM((SUBLANES, 2 * d_ff), _F32))
    else:
        operands.append(prev)
        in_specs.append(pl.BlockSpec((n_seg,) + prev.shape[1:], segs))
    n_state = bsz * (1 if carried else n_tiles * n_seg)
    out_specs = (pl.BlockSpec((1, tile, d_model), row), pl.BlockSpec((n_seg, SUBLANES, 2 * d_ff), segs))
    out_shape = (jax.ShapeDtypeStruct((bsz, seq, d_model), _F32),
                 jax.ShapeDtypeStruct((n_state, SUBLANES, 2 * d_ff), _F32))
    kern = functools.partial(_ffn_kernel, d_ff=d_ff, seg=tile if carried else seg, carried=carried)
    return pl.pallas_call(
        kern, grid=(bsz, n_tiles), in_specs=in_specs, out_specs=out_specs, out_shape=out_shape,
        scratch_shapes=scratch, compiler_params=_params(2), name="ffn")(*operands)


def _rel_rows(table):
    n_low = B_REACH - REL_CLIP + ATTN_SUB - 1
    n_high = REL_ROW - n_low - table.shape[0]
    rows = jnp.concatenate([jnp.broadcast_to(table[-1:], (n_low, table.shape[1])),
                            table[::-1],
                            jnp.broadcast_to(table[:1], (n_high, table.shape[1]))], axis=0)
    return rows.T


def _tile2(g, scale=1.0):
    return (jnp.tile(g, 2) * scale).reshape(1, LANES)


def kernel(x_prompt, x_sample, cache_a_k, cache_a_v, cache_b_k, cache_b_v, cache_ffn_conv, norm1_g, w_in, b_gate, qn_a_g, kn_a_g, qn_b_g, kn_b_g, sinks_a, rel_bias_b, w_out, norm2_g, w_up, conv_w, conv_b, w_down):
    depth = w_in.shape[0]
    bsz, seq, d_model = x_prompt.shape
    dbs, dseq, _ = x_sample.shape
    n_heads = d_model // HEAD_DIM
    assert cache_a_k.shape[2] == A_REACH and cache_b_k.shape[2] == B_REACH
    assert seq % PROJ_TILE == 0 and PROJ_TILE >= B_REACH and PROJ_TILE % ATTN_TILE == 0 and dseq <= CHUNK

    slopes = 2.0 ** (-8.0 * jnp.arange(1, n_heads + 1, dtype=_F32) / n_heads)
    xp, xs = x_prompt, x_sample.reshape(1, dbs * dseq, d_model)
    heads = lambda t, rows: t.reshape(-1, rows, t.shape[-1] // HEAD_DIM, HEAD_DIM)
    w_in, w_out, w_up, w_down = (_to_bf16(w) for w in (w_in, w_out, w_up, w_down))
    flat = lambda c: c.reshape(depth * dbs, c.shape[2], -1)
    caches = [flat(c) for c in (cache_a_k, cache_a_v, cache_b_k, cache_b_v)]
    pk, sk = [], []
    for l in range(depth):
        n1, n2 = norm1_g[l].reshape(1, -1), norm2_g[l].reshape(1, -1)
        bg, cb = b_gate[l].reshape(1, -1), conv_b[l].reshape(1, -1)
        gains = (_tile2(qn_a_g[l], Q_SCALE), _tile2(kn_a_g[l]), _tile2(qn_b_g[l], Q_SCALE), _tile2(kn_b_g[l]))
        rel = _rel_rows(rel_bias_b[l])

        qa, qb, g, ka, va, kb, vb, kat, vat, kbt, vbt = _proj(
            xp, n1, w_in, bg, *gains, layer=l, tile=PROJ_TILE)
        mixed = _attention(slopes, sinks_a[l], rel, qa, qb, g, ka, va, kb, vb)
        xp, st = _ffn(xp, mixed, w_out, n2, w_up, conv_w[l], cb, w_down, layer=l, tile=PROJ_TILE, seg=seq)
        pk.append((heads(kat[:, -A_REACH:], A_REACH), heads(vat[:, -A_REACH:], A_REACH),
                   heads(kbt, B_REACH), heads(vbt, B_REACH), st[:, -(CONV_W - 1):]))

        qa, qb, g, kat, vat, kbt, vbt = _proj(
            xs, n1, w_in, bg, *gains, layer=l, tile=dbs * dseq, emit_kv=False)
        mixed = _sample_attention(slopes, sinks_a[l], rel, qa, qb, g, *caches, kat, vat, kbt, vbt, layer=l, tq=dseq)
        xs, st = _ffn(xs, mixed, w_out, n2, w_up, conv_w[l], cb, w_down, cache_ffn_conv[l],
                      layer=l, tile=dbs * dseq, seg=dseq)
        sk.append((heads(kat, dseq), heads(vat, dseq), heads(kbt, dseq), heads(vbt, dseq),
                   st[:, -(CONV_W - 1):]))

    stk = lambda states, i: jnp.stack([s[i] for s in states])
    return (xp, xs.reshape(dbs, dseq, d_model),
            stk(pk, 0), stk(pk, 1), stk(pk, 2), stk(pk, 3), stk(pk, 4),
            stk(sk, 0), stk(sk, 1), stk(sk, 2), stk(sk, 3), stk(sk, 4))
```

```python
import functools

import jax
import jax.numpy as jnp
from jax import lax
from jax.experimental import pallas as pl
from jax.experimental.pallas import tpu as pltpu

HEAD_DIM = 64
CHUNK = 64
A_GROUP = 4
A_PREV = 2
B_PREV = 8
A_REACH = A_PREV * CHUNK
B_REACH = B_PREV * CHUNK
REL_CLIP = 128
CONV_W = 3
EPS = 1e-6
NEG_INF = -1e30
LOG2E = 1.4426950408889634
Q_SCALE = HEAD_DIM ** -0.5 * LOG2E

LANES = 128
SUBLANES = 8
MXU_WIDTH = 256
VMEM_LIMIT_BYTES = 56 * 1024 * 1024

PROJ_TILE = 512
ATTN_TILE = 512
ATTN_SUB = 128
FF_CHUNK = 256
REL_ROW = 1024

_BF16 = jnp.bfloat16
_F32 = jnp.float32


def _resident(shape, index_map):
    return pl.BlockSpec(shape, index_map, pipeline_mode=pl.Buffered(1))


def _layer_resident(w, layer):
    return pl.BlockSpec((None,) + w.shape[1:], lambda *_: (layer, 0, 0), pipeline_mode=pl.Buffered(1))


def _params(n_axes, flags=None):
    return pltpu.CompilerParams(dimension_semantics=("arbitrary",) * n_axes,
                                vmem_limit_bytes=VMEM_LIMIT_BYTES, flags=flags)


def _rmsnorm_rows(x, g):
    return x * lax.rsqrt(jnp.mean(x * x, axis=-1, keepdims=True) + EPS) * g


def _lane_is_low():
    return lax.broadcasted_iota(jnp.int32, (1, LANES), 1) < HEAD_DIM


def _headnorm_slab(z, g2):
    low = _lane_is_low()
    sq = z * z
    s_lo = jnp.sum(jnp.where(low, sq, 0.0), axis=-1, keepdims=True)
    s_hi = jnp.sum(jnp.where(low, 0.0, sq), axis=-1, keepdims=True)
    ms = jnp.where(low, s_lo, s_hi) * (1.0 / HEAD_DIM)
    return z * lax.rsqrt(ms + EPS) * g2


CAST_BLOCK_BYTES = 6 * 1024 * 1024


def _cast_kernel(x_ref, o_ref):
    o_ref[...] = x_ref[...].astype(o_ref.dtype)


def _to_bf16(w):
    depth, rows, cols = w.shape
    packing = 2 * SUBLANES
    fits = [r for r in range(packing, rows + 1, packing) if rows % r == 0 and r * cols * 4 <= CAST_BLOCK_BYTES]
    rb = max(fits)
    spec = pl.BlockSpec((1, rb, cols), lambda l, i: (l, i, 0))
    return pl.pallas_call(_cast_kernel, grid=(depth, rows // rb), in_specs=[spec], out_specs=spec,
                          out_shape=jax.ShapeDtypeStruct(w.shape, _BF16), compiler_params=_params(2),
                          name="cast")(w)


def _proj_kernel(x_ref, n1_ref, w_ref, bg_ref, gqa_ref, gka_ref, gqb_ref, gkb_ref, qa_ref, qb_ref, g_ref, *outs,
                 d_model, a_kv, emit_kv):
    ka_ref, va_ref, kb_ref, vb_ref = outs[:4] if emit_kv else (None,) * 4
    kat_ref, vat_ref, kbt_ref, vbt_ref = outs[-4:]
    o_ka = d_model
    o_va = o_ka + a_kv
    o_qb = o_va + a_kv
    o_kb = o_qb + d_model
    o_vb = o_kb + d_model
    o_g = o_vb + d_model

    h = _rmsnorm_rows(x_ref[0], n1_ref[...]).astype(_BF16)
    low = _lane_is_low()

    def slabs(base, width):
        for c in range(0, width, MXU_WIDTH):
            z = jnp.dot(h, w_ref[:, base + c:base + c + MXU_WIDTH], preferred_element_type=_F32)
            for t in range(MXU_WIDTH // LANES):
                yield c // LANES + t, z[:, t * LANES:(t + 1) * LANES]

    def emit(out_ref, base, width, gain_ref=None, tail_ref=None, twice=False, feature_major=False):
        def put(idx, y):
            if feature_major:
                yt = y.T.astype(out_ref.dtype)
                for t in range(out_ref.shape[1]):
                    out_ref[0, t, idx] = yt[:, t * ATTN_TILE:(t + 1) * ATTN_TILE]
            else:
                out_ref[0, idx] = y.astype(out_ref.dtype)

        for s, y in slabs(base, width):
            if gain_ref is not None:
                y = _headnorm_slab(y, gain_ref[...])
            if tail_ref is not None:
                tail_ref[0, :, s * LANES:(s + 1) * LANES] = y
            if out_ref is None:
                continue
            if twice:
                y_sw = pltpu.roll(y, HEAD_DIM, 1)
                put(2 * s, jnp.where(low, y, y_sw))
                put(2 * s + 1, jnp.where(low, y_sw, y))
            else:
                put(s, y)

    emit(qa_ref, 0, d_model, gqa_ref)
    emit(ka_ref, o_ka, a_kv, gka_ref, kat_ref, twice=True, feature_major=True)
    emit(qb_ref, o_qb, d_model, gqb_ref)
    emit(kb_ref, o_kb, d_model, gkb_ref, kbt_ref, feature_major=True)
    for s, y in slabs(o_g, 2 * d_model):
        c = s * LANES
        g_ref[0, s] = jax.nn.sigmoid(y + bg_ref[:, c:c + LANES]).astype(g_ref.dtype)
    emit(va_ref, o_va, a_kv, None, vat_ref, twice=True)
    emit(vb_ref, o_vb, d_model, None, vbt_ref)


def _proj(x, n1, w_in, b_gate, gqa, gka, gqb, gkb, *, layer, tile, emit_kv=True):
    bsz, seq, d_model = x.shape
    a_kv = d_model // A_GROUP
    ns_d, ns_kv = d_model // LANES, a_kv // LANES

    def row(b, j):
        return (b, j, 0)

    def slab_row(b, j):
        return (b, 0, j, 0)

    def key_blocks(b, j):
        return (b, j, 0, 0, 0)

    def const2(b, j):
        return (0, 0)

    def tail(b, j):
        return (b, 0, 0)

    bf = lambda ns: jax.ShapeDtypeStruct((bsz, ns, seq, LANES), _BF16)
    bf_t = lambda ns: jax.ShapeDtypeStruct((bsz, seq // ATTN_TILE, ns, LANES, ATTN_TILE), _BF16)
    f32 = lambda ns: jax.ShapeDtypeStruct((bsz, tile, ns * LANES), _F32)
    blk = lambda ns: pl.BlockSpec((1, ns, tile, LANES), slab_row)
    blk_t = lambda ns: pl.BlockSpec((1, tile // ATTN_TILE, ns, LANES, ATTN_TILE), key_blocks)
    blk_tail = lambda ns: pl.BlockSpec((1, tile, ns * LANES), tail)
    out_shape = [bf(ns_d), bf(ns_d), bf(2 * ns_d)]
    out_specs = [blk(ns_d), blk(ns_d), blk(2 * ns_d)]
    if emit_kv:
        out_shape += [bf_t(2 * ns_kv), bf(2 * ns_kv), bf_t(ns_d), bf(ns_d)]
        out_specs += [blk_t(2 * ns_kv), blk(2 * ns_kv), blk_t(ns_d), blk(ns_d)]
    out_shape += [f32(ns_kv), f32(ns_kv), f32(ns_d), f32(ns_d)]
    out_specs += [blk_tail(ns_kv), blk_tail(ns_kv), blk_tail(ns_d), blk_tail(ns_d)]
    in_specs = [pl.BlockSpec((1, tile, d_model), row),
                _resident((1, d_model), const2),
                _layer_resident(w_in, layer),
                _resident((1, 2 * d_model), const2),
                _resident((1, LANES), const2), _resident((1, LANES), const2),
                _resident((1, LANES), const2), _resident((1, LANES), const2)]
    kern = functools.partial(_proj_kernel, d_model=d_model, a_kv=a_kv, emit_kv=emit_kv)
    return pl.pallas_call(kern, grid=(bsz, seq // tile), in_specs=in_specs, out_specs=tuple(out_specs),
                          out_shape=tuple(out_shape), compiler_params=_params(2), name="proj")(
                              x, n1, w_in, b_gate, gqa, gka, gqb, gkb)


def _band_mask(tq, w, n_prev):
    shift = CHUNK.bit_length() - 1
    qc = lax.shift_right_logical(lax.broadcasted_iota(jnp.int32, (tq, w), 0), shift)
    kc = lax.shift_right_logical(lax.broadcasted_iota(jnp.int32, (tq, w), 1), shift)
    d = kc - qc
    return (d >= 0) & (d <= n_prev)


def _window(reach, tq):
    return -(-(reach + tq) // LANES) * LANES


def _stack_heads(qs):
    low = _lane_is_low()
    zero = jnp.zeros_like(qs)
    return jnp.concatenate([jnp.where(low, qs, zero), jnp.where(low, zero, qs)], axis=0)


def _with_ones(v):
    return jnp.concatenate([v, jnp.ones_like(v)], axis=1)


def _normalised(o2, extra_den, rows):
    den = o2[:, LANES:]
    if extra_den is not None:
        den = den + extra_den
    o = o2[:, :LANES] / den
    return jnp.where(_lane_is_low(), o[:rows], o[rows:])


def _init_bias(slopes_ref, rel_ref, bias_a, bias_b, ts):
    n_slabs, _, wa = bias_a.shape
    wb = bias_b.shape[2]
    qpos = lax.broadcasted_iota(jnp.int32, (ts, wa), 0)
    kpos = lax.broadcasted_iota(jnp.int32, (ts, wa), 1) - A_REACH
    dist = jnp.abs(qpos - kpos).astype(_F32)
    band_a = _band_mask(ts, wa, A_PREV)
    band_b = _band_mask(ts, wb, B_PREV)
    for h in range(2 * n_slabs):
        rows_h = slice((h % 2) * ts, (h % 2 + 1) * ts)
        bias_a[h // 2, rows_h, :] = jnp.where(band_a, (-LOG2E * slopes_ref[h]) * dist, NEG_INF)
        rows = jnp.broadcast_to(rel_ref[h:h + 1, :], (ts, REL_ROW))
        rows = pltpu.roll(rows, REL_ROW - ATTN_SUB + 1, 1, stride=1, stride_axis=0)
        bias_b[h // 2, rows_h, :] = jnp.where(band_b, LOG2E * rows[:, :wb], NEG_INF)


def _attn_kernel(*refs, tq, ts, n_blocks, n_slabs):
    na, nb = n_blocks
    slopes_ref, sinks_ref, rel_ref, qa_ref, qb_ref, g_ref = refs[:6]
    kv = list(refs[6:6 + 2 * (na + nb)])
    ka_refs, va_refs, kb_refs, vb_refs = kv[:na], kv[na:2 * na], kv[2 * na:2 * na + nb], kv[2 * na + nb:]
    o_ref, bias_a, bias_b = refs[6 + len(kv):9 + len(kv)]
    bufs = refs[9 + len(kv):]
    sa_ref, sb_ref, pa_ref, pb_ref, da_ref = (bufs[2 * n:2 * n + 2] for n in range(5))
    wa, wb = bias_a.shape[2], bias_b.shape[2]
    b = pl.program_id(0)
    i = pl.program_id(1)

    @pl.when((b == 0) & (i == 0))
    def _():
        _init_bias(slopes_ref, rel_ref, bias_a, bias_b, ts)

    row_is_lo = lax.broadcasted_iota(jnp.int32, (2 * ts, 1), 0) < ts
    col_a = lax.broadcasted_iota(jnp.int32, (1, wa), 1)
    col_b = lax.broadcasted_iota(jnp.int32, (1, wb), 1)

    def window(blocks, slab, u, reach, feature_major):
        if feature_major:
            cut = lambda r, lo, hi: r[0, 0, slab, :, lo:hi]
        else:
            cut = lambda r, lo, hi: r[0, slab, lo:hi, :]
        lo, hi = u * ts - reach, (u + 1) * ts
        pieces = []
        for j, r in enumerate(blocks):
            base = (j - len(blocks) + 1) * tq
            if max(lo, base) < min(hi, base + tq):
                pieces.append(cut(r, max(lo, base) - base, min(hi, base + tq) - base))
        return pieces[0] if len(pieces) == 1 else jnp.concatenate(pieces, axis=1 if feature_major else 0)

    def softmax_weights(s, sink_col):
        m = jnp.max(s, axis=-1, keepdims=True)
        if sink_col is not None:
            m = jnp.maximum(m, sink_col)
        p = jnp.exp2(s - m).astype(_BF16)
        return p, None if sink_col is None else jnp.broadcast_to(jnp.exp2(sink_col - m), (2 * ts, LANES))

    def run(masked):
        valid = []
        for u in range(tq // ts):
            if masked:
                start = i * tq + u * ts
                valid.append((jnp.where(col_a >= A_REACH - start, 0.0, NEG_INF),
                              jnp.where(col_b >= B_REACH - start, 0.0, NEG_INF)))
            else:
                valid.append((None, None))

        def scores(k, u, par):
            rows_u = slice(u * ts, (u + 1) * ts)
            sa_ref[par][...] = jnp.dot(_stack_heads(qa_ref[0, k, rows_u]), window(ka_refs, k // 2, u, A_REACH, True),
                                       preferred_element_type=_F32)
            sb_ref[par][...] = jnp.dot(_stack_heads(qb_ref[0, k, rows_u]), window(kb_refs, k, u, B_REACH, True),
                                       preferred_element_type=_F32)

        def softmax(k, u, par):
            valid_a, valid_b = valid[u]
            sink_col = LOG2E * jnp.where(row_is_lo, sinks_ref[2 * k], sinks_ref[2 * k + 1])
            s = sa_ref[par][...] + bias_a[k]
            pa_ref[par][...], da_ref[par][...] = softmax_weights(s if valid_a is None else s + valid_a, sink_col)
            s = sb_ref[par][...] + bias_b[k]
            pb_ref[par][...], _ = softmax_weights(s if valid_b is None else s + valid_b, None)

        def values(k, u, par):
            rows_u = slice(u * ts, (u + 1) * ts)
            oa = jnp.dot(pa_ref[par][...], _with_ones(window(va_refs, k // 2, u, A_REACH, False)),
                         preferred_element_type=_F32)
            ob = jnp.dot(pb_ref[par][...], _with_ones(window(vb_refs, k, u, B_REACH, False)),
                         preferred_element_type=_F32)
            mixed = (g_ref[0, k, rows_u].astype(_F32) * _normalised(oa, da_ref[par][...], ts)
                     + g_ref[0, n_slabs + k, rows_u].astype(_F32) * _normalised(ob, None, ts))
            o_ref[0, k, rows_u] = mixed.astype(o_ref.dtype)

        n_sub = tq // ts

        def step(k, u, first_slab=False):
            for back, stage in ((2, values), (1, softmax)):
                ku, uu = (k, u - back) if u >= back else (k - 1, u - back + n_sub)
                if not (first_slab and u < back):
                    stage(ku, uu, uu % 2)
            scores(k, u, u % 2)

        for u in range(n_sub):
            step(0, u, first_slab=True)

        def body(k, carry):
            for u in range(n_sub):
                step(k, u)
            return carry

        lax.fori_loop(1, n_slabs, body, 0, unroll=2)
        values(n_slabs - 1, n_sub - 2, 0)
        softmax(n_slabs - 1, n_sub - 1, 1)
        values(n_slabs - 1, n_sub - 1, 1)

    n_early = B_REACH // tq
    pl.when(i < n_early)(lambda: run(True))
    pl.when(i >= n_early)(lambda: run(False))


def _attention(slopes, sinks, rel_rows, qa, qb, gates, ka, va, kb, vb):
    bsz, n_slabs, seq, _ = qa.shape
    tq, ts = ATTN_TILE, ATTN_SUB
    assert tq % (2 * ts) == 0 and seq % tq == 0 and B_REACH % tq == 0
    n_blocks = (1 + -(-A_REACH // tq), 1 + B_REACH // tq)

    def row(b, i):
        return (b, 0, i, 0)

    def const2(b, i):
        return (0, 0)

    def kv_specs(t, n, feature_major):
        if feature_major:
            return [pl.BlockSpec((1, 1) + t.shape[2:], lambda b, i, d=d: (b, jnp.maximum(i - d, 0), 0, 0, 0))
                    for d in range(n - 1, -1, -1)]
        return [pl.BlockSpec((1, t.shape[1], tq, LANES), lambda b, i, d=d: (b, 0, jnp.maximum(i - d, 0), 0))
                for d in range(n - 1, -1, -1)]

    smem = pl.BlockSpec(memory_space=pltpu.SMEM)
    rows = lambda ns: pl.BlockSpec((1, ns, tq, LANES), row)
    na, nb = n_blocks
    wa, wb = A_REACH + ts, B_REACH + ts
    in_specs = ([smem, smem, _resident(rel_rows.shape, const2), rows(n_slabs), rows(n_slabs), rows(2 * n_slabs)]
                + kv_specs(ka, na, True) + kv_specs(va, na, False) + kv_specs(kb, nb, True) + kv_specs(vb, nb, False))
    operands = [slopes, sinks, rel_rows, qa, qb, gates] + [ka] * na + [va] * na + [kb] * nb + [vb] * nb
    kern = functools.partial(_attn_kernel, tq=tq, ts=ts, n_blocks=n_blocks, n_slabs=n_slabs)
    return pl.pallas_call(
        kern, grid=(bsz, seq // tq), in_specs=in_specs, out_specs=rows(n_slabs),
        out_shape=jax.ShapeDtypeStruct(qa.shape, _BF16),
        scratch_shapes=[pltpu.VMEM((n_slabs, 2 * ts, wa), _F32), pltpu.VMEM((n_slabs, 2 * ts, wb), _F32)]
        + [pltpu.VMEM((2 * ts, w), dt)
           for w, dt in ((wa, _F32), (wb, _F32), (wa, _BF16), (wb, _BF16), (LANES, _F32)) for _ in range(2)],
        compiler_params=_params(2), name="attn")(*operands)


def _sample_attn_kernel(slopes_ref, sinks_ref, rel_ref, qa_ref, qb_ref, g_ref,
                        cak_ref, cav_ref, cbk_ref, cbv_ref, nak_ref, nav_ref, nbk_ref, nbv_ref,
                        o_ref, bias_a, bias_b, *, tq, n_slabs):
    @pl.when(pl.program_id(0) == 0)
    def _():
        _init_bias(slopes_ref, rel_ref, bias_a, bias_b, tq)

    low = _lane_is_low()
    row_is_lo = lax.broadcasted_iota(jnp.int32, (2 * tq, 1), 0) < tq

    def both_halves(x, half):
        sw = pltpu.roll(x, HEAD_DIM, 1)
        return (jnp.where(low, x, sw) if half == 0 else jnp.where(low, sw, x)).astype(_BF16)

    def attend(qs, parts, bias, sink_col):
        qq = _stack_heads(qs)
        ss = [lax.dot_general(qq, k, (((1,), (1,)), ((), ())), preferred_element_type=_F32)
              + bias[:, c0:c0 + k.shape[0]] for k, _, c0 in parts]
        m = functools.reduce(jnp.maximum, [jnp.max(s, axis=-1, keepdims=True) for s in ss])
        if sink_col is not None:
            m = jnp.maximum(m, sink_col)
        o2 = sum(jnp.dot(jnp.exp2(s - m).astype(_BF16), _with_ones(v), preferred_element_type=_F32)
                 for s, (_, v, _) in zip(ss, parts))
        return _normalised(o2, None if sink_col is None else jnp.exp2(sink_col - m), tq)

    parts_a = None
    for s in range(n_slabs):
        kvh = s // 2
        if s % 2 == 0:
            la = slice(kvh // 2 * LANES, (kvh // 2 + 1) * LANES)
            dup = lambda r: both_halves(r[0, :, la], kvh % 2)
            parts_a = [(dup(cak_ref), dup(cav_ref), 0), (dup(nak_ref), dup(nav_ref), A_REACH)]
        sink_col = LOG2E * jnp.where(row_is_lo, sinks_ref[2 * s], sinks_ref[2 * s + 1])
        oa = attend(qa_ref[0, s], parts_a, bias_a[s], sink_col)
        lb = slice(s * LANES, (s + 1) * LANES)
        cut = lambda r: r[0, :, lb].astype(_BF16)
        ob = attend(qb_ref[0, s], [(cut(cbk_ref), cut(cbv_ref), 0), (cut(nbk_ref), cut(nbv_ref), B_REACH)],
                    bias_b[s], None)
        mixed = g_ref[0, s].astype(_F32) * oa + g_ref[0, n_slabs + s].astype(_F32) * ob
        o_ref[0, s] = mixed.astype(o_ref.dtype)


def _sample_attention(slopes, sinks, rel_rows, qa, qb, gates, cache_a_k, cache_a_v, cache_b_k, cache_b_v,
                      new_a_k, new_a_v, new_b_k, new_b_v, *, layer, tq):
    _, n_slabs, rows, _ = qa.shape
    n_batches = rows // tq

    def q_rows(b):
        return (0, 0, b, 0)

    def cached(b):
        return (layer * n_batches + b, 0, 0)

    def new_rows(b):
        return (0, b, 0)

    smem = pl.BlockSpec(memory_space=pltpu.SMEM)
    q_blk = lambda ns: pl.BlockSpec((1, ns, tq, LANES), q_rows)
    cache_blk = lambda t: pl.BlockSpec((1,) + t.shape[1:], cached)
    new_blk = lambda t: pl.BlockSpec((1, tq, t.shape[2]), new_rows)
    caches = (cache_a_k, cache_a_v, cache_b_k, cache_b_v)
    news = (new_a_k, new_a_v, new_b_k, new_b_v)
    in_specs = ([smem, smem, _resident(rel_rows.shape, lambda b: (0, 0)), q_blk(n_slabs), q_blk(n_slabs),
                 q_blk(2 * n_slabs)] + [cache_blk(t) for t in caches] + [new_blk(t) for t in news])
    kern = functools.partial(_sample_attn_kernel, tq=tq, n_slabs=n_slabs)
    return pl.pallas_call(
        kern, grid=(n_batches,), in_specs=in_specs, out_specs=q_blk(n_slabs),
        out_shape=jax.ShapeDtypeStruct(qa.shape, _BF16),
        scratch_shapes=[pltpu.VMEM((n_slabs, 2 * tq, _window(A_REACH, tq)), _F32),
                        pltpu.VMEM((n_slabs, 2 * tq, _window(B_REACH, tq)), _F32)],
        compiler_params=_params(1), name="sample_attn")(slopes, sinks, rel_rows, qa, qb, gates, *caches, *news)


def _gelu(x):
    return 0.5 * x * (1.0 + lax.erf(x * (2.0 ** -0.5)))


def _ffn_kernel(*refs, d_ff, seg, carried):
    if carried:
        x_ref, m_ref, wo_ref, n2_ref, wu_ref, cw_ref, cb_ref, wd_ref, y_ref, st_ref, carry, act = refs
    else:
        x_ref, m_ref, wo_ref, n2_ref, wu_ref, cw_ref, cb_ref, wd_ref, prev_ref, y_ref, st_ref, act = refs
    rows = x_ref.shape[1]
    n_seg = rows // seg

    if carried:
        @pl.when(pl.program_id(1) == 0)
        def _():
            carry[...] = jnp.zeros(carry.shape, carry.dtype)

    mixed = jnp.concatenate([m_ref[0, s] for s in range(m_ref.shape[1])], axis=-1)
    x1 = x_ref[0] + jnp.dot(mixed, wo_ref[...], preferred_element_type=_F32)
    h = _rmsnorm_rows(x1, n2_ref[...]).astype(_BF16)
    r = jnp.bitwise_and(lax.broadcasted_iota(jnp.int32, (rows, 1), 0), seg - 1)

    def conv(col):
        u = jnp.dot(h, wu_ref[:, col:col + FF_CHUNK], preferred_element_type=_F32)
        cs = slice(col, col + FF_CHUNK)
        if carried:
            p0, p1 = carry[SUBLANES - 2:SUBLANES - 1, cs], carry[SUBLANES - 1:SUBLANES, cs]
            carry[:, cs] = u[rows - SUBLANES:, :]
        else:
            per_seg = lambda j: jnp.concatenate(
                [jnp.broadcast_to(prev_ref[g, j:j + 1, cs], (seg, FF_CHUNK)) for g in range(n_seg)], axis=0)
            p0, p1 = per_seg(0), per_seg(1)
        for g in range(n_seg):
            st_ref[g, :, cs] = u[(g + 1) * seg - SUBLANES:(g + 1) * seg, :]
        u1 = jnp.where(r == 0, p1, pltpu.roll(u, 1, 0))
        u2 = jnp.where(r == 0, p0, jnp.where(r == 1, p1, pltpu.roll(u, 2, 0)))
        return cb_ref[:, cs] + cw_ref[0:1, cs] * u2 + cw_ref[1:2, cs] * u1 + cw_ref[2:3, cs] * u

    for c in range(0, d_ff, FF_CHUNK):
        a = conv(c)
        g = conv(d_ff + c)
        act[:, c:c + FF_CHUNK] = (_gelu(a) * g).astype(act.dtype)

    y_ref[0] = x1 + jnp.dot(act[...], wd_ref[...], preferred_element_type=_F32)


def _ffn(x, mixed, w_out, n2, w_up, conv_w, conv_b, w_down, prev=None, *, layer, tile, seg):
    bsz, seq, d_model = x.shape
    d_ff = w_down.shape[1]
    n_tiles = seq // tile
    carried = prev is None
    n_seg = 1 if carried else tile // seg
    assert seg & (seg - 1) == 0 and (seg == seq if carried else tile % seg == 0)

    def row(b, i):
        return (b, i, 0)

    def slab_row(b, i):
        return (b, 0, i, 0)

    def segs(b, i):
        return (b if carried else b * n_tiles + i, 0, 0)

    def const2(b, i):
        return (0, 0)

    operands = [x, mixed, w_out, n2, w_up, conv_w, conv_b, w_down]
    in_specs = [pl.BlockSpec((1, tile, d_model), row),
                pl.BlockSpec((1, mixed.shape[1], tile, LANES), slab_row),
                _layer_resident(w_out, layer), _resident((1, d_model), const2),
                _layer_resident(w_up, layer), _resident(conv_w.shape, const2),
                _resident((1, 2 * d_ff), const2), _layer_resident(w_down, layer)]
    scratch = [pltpu.VMEM((tile, d_ff), _BF16)]
    if carried:
        scratch.insert(0, pltpu.VMEM((SUBLANES, 2 * d_ff), _F32))
    else:
        operands.append(prev)
        in_specs.append(pl.BlockSpec((n_seg,) + prev.shape[1:], segs))
    n_state = bsz * (1 if carried else n_tiles * n_seg)
    out_specs = (pl.BlockSpec((1, tile, d_model), row), pl.BlockSpec((n_seg, SUBLANES, 2 * d_ff), segs))
    out_shape = (jax.ShapeDtypeStruct((bsz, seq, d_model), _F32),
                 jax.ShapeDtypeStruct((n_state, SUBLANES, 2 * d_ff), _F32))
    kern = functools.partial(_ffn_kernel, d_ff=d_ff, seg=tile if carried else seg, carried=carried)
    return pl.pallas_call(
        kern, grid=(bsz, n_tiles), in_specs=in_specs, out_specs=out_specs, out_shape=out_shape,
        scratch_shapes=scratch, compiler_params=_params(2), name="ffn")(*operands)


def _rel_rows(table):
    n_low = B_REACH - REL_CLIP + ATTN_SUB - 1
    n_high = REL_ROW - n_low - table.shape[0]
    rows = jnp.concatenate([jnp.broadcast_to(table[-1:], (n_low, table.shape[1])),
                            table[::-1],
                            jnp.broadcast_to(table[:1], (n_high, table.shape[1]))], axis=0)
    return rows.T


def _tile2(g, scale=1.0):
    return (jnp.tile(g, 2) * scale).reshape(1, LANES)


def kernel(x_prompt, x_sample, cache_a_k, cache_a_v, cache_b_k, cache_b_v, cache_ffn_conv, norm1_g, w_in, b_gate, qn_a_g, kn_a_g, qn_b_g, kn_b_g, sinks_a, rel_bias_b, w_out, norm2_g, w_up, conv_w, conv_b, w_down):
    depth = w_in.shape[0]
    bsz, seq, d_model = x_prompt.shape
    dbs, dseq, _ = x_sample.shape
    n_heads = d_model // HEAD_DIM
    assert cache_a_k.shape[2] == A_REACH and cache_b_k.shape[2] == B_REACH
    assert seq % PROJ_TILE == 0 and PROJ_TILE >= B_REACH and PROJ_TILE % ATTN_TILE == 0 and dseq <= CHUNK
    slopes = 2.0 ** (-8.0 * jnp.arange(1, n_heads + 1, dtype=_F32) / n_heads)
    xp, xs = x_prompt, x_sample.reshape(1, dbs * dseq, d_model)
    heads = lambda t, rows: t.reshape(-1, rows, t.shape[-1] // HEAD_DIM, HEAD_DIM)
    w_in, w_out, w_up, w_down = (_to_bf16(w) for w in (w_in, w_out, w_up, w_down))
    flat = lambda c: c.reshape(depth * dbs, c.shape[2], -1)
    caches = [flat(c) for c in (cache_a_k, cache_a_v, cache_b_k, cache_b_v)]
    pk, sk = [], []
    for l in range(depth):
        n1, n2 = norm1_g[l].reshape(1, -1), norm2_g[l].reshape(1, -1)
        bg, cb = b_gate[l].reshape(1, -1), conv_b[l].reshape(1, -1)
        gains = (_tile2(qn_a_g[l], Q_SCALE), _tile2(kn_a_g[l]), _tile2(qn_b_g[l], Q_SCALE), _tile2(kn_b_g[l]))
        rel = _rel_rows(rel_bias_b[l])

        qa, qb, g, ka, va, kb, vb, kat, vat, kbt, vbt = _proj(
            xp, n1, w_in, bg, *gains, layer=l, tile=PROJ_TILE)
        mixed = _attention(slopes, sinks_a[l], rel, qa, qb, g, ka, va, kb, vb)
        xp, st = _ffn(xp, mixed, w_out, n2, w_up, conv_w[l], cb, w_down, layer=l, tile=PROJ_TILE, seg=seq)
        pk.append((heads(kat[:, -A_REACH:], A_REACH), heads(vat[:, -A_REACH:], A_REACH),
                   heads(kbt, B_REACH), heads(vbt, B_REACH), st[:, -(CONV_W - 1):]))

        qa, qb, g, kat, vat, kbt, vbt = _proj(
            xs, n1, w_in, bg, *gains, layer=l, tile=dbs * dseq, emit_kv=False)
        mixed = _sample_attention(slopes, sinks_a[l], rel, qa, qb, g, *caches, kat, vat, kbt, vbt, layer=l, tq=dseq)
        xs, st = _ffn(xs, mixed, w_out, n2, w_up, conv_w[l], cb, w_down, cache_ffn_conv[l],
                      layer=l, tile=dbs * dseq, seg=dseq)
        sk.append((heads(kat, dseq), heads(vat, dseq), heads(kbt, dseq), heads(vbt, dseq),
                   st[:, -(CONV_W - 1):]))

    stk = lambda states, i: jnp.stack([s[i] for s in states])
    return (xp, xs.reshape(dbs, dseq, d_model),
            stk(pk, 0), stk(pk, 1), stk(pk, 2), stk(pk, 3), stk(pk, 4),
            stk(sk, 0), stk(sk, 1), stk(sk, 2), stk(sk, 3), stk(sk, 4))
```

```python
import functools

import jax
import jax.numpy as jnp
from jax import lax
from jax.experimental import pallas as pl
from jax.experimental.pallas import tpu as pltpu

HEAD_DIM = 64
CHUNK = 64
A_GROUP = 4
A_PREV = 2
B_PREV = 8
A_REACH = A_PREV * CHUNK
B_REACH = B_PREV * CHUNK
REL_CLIP = 128
CONV_W = 3
EPS = 1e-6
NEG_INF = -1e30
LOG2E = 1.4426950408889634
Q_SCALE = HEAD_DIM ** -0.5 * LOG2E

LANES = 128
SUBLANES = 8
MXU_WIDTH = 256
VMEM_LIMIT_BYTES = 56 * 1024 * 1024

PROJ_TILE = 512
ATTN_TILE = 512
ATTN_SUB = 128
FF_CHUNK = 256
REL_ROW = 1024

_BF16 = jnp.bfloat16
_F32 = jnp.float32


def _resident(shape, index_map):
    return pl.BlockSpec(shape, index_map, pipeline_mode=pl.Buffered(1))


def _layer_resident(w, layer):
    return pl.BlockSpec((None,) + w.shape[1:], lambda *_: (layer, 0, 0), pipeline_mode=pl.Buffered(1))


def _params(n_axes, flags=None):
    return pltpu.CompilerParams(dimension_semantics=("arbitrary",) * n_axes,
                                vmem_limit_bytes=VMEM_LIMIT_BYTES, flags=flags)


def _rmsnorm_rows(x, g):
    return x * lax.rsqrt(jnp.mean(x * x, axis=-1, keepdims=True) + EPS) * g


def _lane_is_low():
    return lax.broadcasted_iota(jnp.int32, (1, LANES), 1) < HEAD_DIM


def _headnorm_slab(z, g2):
    low = _lane_is_low()
    sq = z * z
    s_lo = jnp.sum(jnp.where(low, sq, 0.0), axis=-1, keepdims=True)
    s_hi = jnp.sum(jnp.where(low, 0.0, sq), axis=-1, keepdims=True)
    ms = jnp.where(low, s_lo, s_hi) * (1.0 / HEAD_DIM)
    return z * lax.rsqrt(ms + EPS) * g2


CAST_BLOCK_BYTES = 6 * 1024 * 1024


def _cast_kernel(x_ref, o_ref):
    o_ref[...] = x_ref[...].astype(o_ref.dtype)


def _to_bf16(w):
    depth, rows, cols = w.shape
    packing = 2 * SUBLANES
    fits = [r for r in range(packing, rows + 1, packing) if rows % r == 0 and r * cols * 4 <= CAST_BLOCK_BYTES]
    rb = max(fits)
    spec = pl.BlockSpec((1, rb, cols), lambda l, i: (l, i, 0))
    return pl.pallas_call(_cast_kernel, grid=(depth, rows // rb), in_specs=[spec], out_specs=spec,
                          out_shape=jax.ShapeDtypeStruct(w.shape, _BF16), compiler_params=_params(2),
                          name="cast")(w)


def _proj_kernel(x_ref, n1_ref, w_ref, bg_ref, gqa_ref, gka_ref, gqb_ref, gkb_ref, qa_ref, qb_ref, g_ref, *outs,
                 d_model, a_kv, emit_kv):
    ka_ref, va_ref, kb_ref, vb_ref = outs[:4] if emit_kv else (None,) * 4
    kat_ref, vat_ref, kbt_ref, vbt_ref = outs[-4:]
    o_ka = d_model
    o_va = o_ka + a_kv
    o_qb = o_va + a_kv
    o_kb = o_qb + d_model
    o_vb = o_kb + d_model
    o_g = o_vb + d_model

    h = _rmsnorm_rows(x_ref[0], n1_ref[...]).astype(_BF16)
    low = _lane_is_low()

    def slabs(base, width):
        for c in range(0, width, MXU_WIDTH):
            z = jnp.dot(h, w_ref[:, base + c:base + c + MXU_WIDTH], preferred_element_type=_F32)
            for t in range(MXU_WIDTH // LANES):
                yield c // LANES + t, z[:, t * LANES:(t + 1) * LANES]

    def emit(out_ref, base, width, gain_ref=None, tail_ref=None, twice=False, feature_major=False):
        def put(idx, y):
            if feature_major:
                yt = y.T.astype(out_ref.dtype)
                for t in range(out_ref.shape[1]):
                    out_ref[0, t, idx] = yt[:, t * ATTN_TILE:(t + 1) * ATTN_TILE]
            else:
                out_ref[0, idx] = y.astype(out_ref.dtype)

        for s, y in slabs(base, width):
            if gain_ref is not None:
                y = _headnorm_slab(y, gain_ref[...])
            if tail_ref is not None:
                tail_ref[0, :, s * LANES:(s + 1) * LANES] = y
            if out_ref is None:
                continue
            if twice:
                y_sw = pltpu.roll(y, HEAD_DIM, 1)
                put(2 * s, jnp.where(low, y, y_sw))
                put(2 * s + 1, jnp.where(low, y_sw, y))
            else:
                put(s, y)

    emit(qa_ref, 0, d_model, gqa_ref)
    emit(ka_ref, o_ka, a_kv, gka_ref, kat_ref, twice=True, feature_major=True)
    emit(qb_ref, o_qb, d_model, gqb_ref)
    emit(kb_ref, o_kb, d_model, gkb_ref, kbt_ref, feature_major=True)
    for s, y in slabs(o_g, 2 * d_model):
        c = s * LANES
        g_ref[0, s] = jax.nn.sigmoid(y + bg_ref[:, c:c + LANES]).astype(g_ref.dtype)
    emit(va_ref, o_va, a_kv, None, vat_ref, twice=True)
    emit(vb_ref, o_vb, d_model, None, vbt_ref)


def _proj(x, n1, w_in, b_gate, gqa, gka, gqb, gkb, *, layer, tile, emit_kv=True):
    bsz, seq, d_model = x.shape
    a_kv = d_model // A_GROUP
    ns_d, ns_kv = d_model // LANES, a_kv // LANES

    def row(b, j):
        return (b, j, 0)

    def slab_row(b, j):
        return (b, 0, j, 0)

    def key_blocks(b, j):
        return (b, j, 0, 0, 0)

    def const2(b, j):
        return (0, 0)

    def tail(b, j):
        return (b, 0, 0)

    bf = lambda ns: jax.ShapeDtypeStruct((bsz, ns, seq, LANES), _BF16)
    bf_t = lambda ns: jax.ShapeDtypeStruct((bsz, seq // ATTN_TILE, ns, LANES, ATTN_TILE), _BF16)
    f32 = lambda ns: jax.ShapeDtypeStruct((bsz, tile, ns * LANES), _F32)
    blk = lambda ns: pl.BlockSpec((1, ns, tile, LANES), slab_row)
    blk_t = lambda ns: pl.BlockSpec((1, tile // ATTN_TILE, ns, LANES, ATTN_TILE), key_blocks)
    blk_tail = lambda ns: pl.BlockSpec((1, tile, ns * LANES), tail)
    out_shape = [bf(ns_d), bf(ns_d), bf(2 * ns_d)]
    out_specs = [blk(ns_d), blk(ns_d), blk(2 * ns_d)]
    if emit_kv:
        out_shape += [bf_t(2 * ns_kv), bf(2 * ns_kv), bf_t(ns_d), bf(ns_d)]
        out_specs += [blk_t(2 * ns_kv), blk(2 * ns_kv), blk_t(ns_d), blk(ns_d)]
    out_shape += [f32(ns_kv), f32(ns_kv), f32(ns_d), f32(ns_d)]
    out_specs += [blk_tail(ns_kv), blk_tail(ns_kv), blk_tail(ns_d), blk_tail(ns_d)]
    in_specs = [pl.BlockSpec((1, tile, d_model), row),
                _resident((1, d_model), const2),
                _layer_resident(w_in, layer),
                _resident((1, 2 * d_model), const2),
                _resident((1, LANES), const2), _resident((1, LANES), const2),
                _resident((1, LANES), const2), _resident((1, LANES), const2)]
    kern = functools.partial(_proj_kernel, d_model=d_model, a_kv=a_kv, emit_kv=emit_kv)
    return pl.pallas_call(kern, grid=(bsz, seq // tile), in_specs=in_specs, out_specs=tuple(out_specs),
                          out_shape=tuple(out_shape), compiler_params=_params(2), name="proj")(
                              x, n1, w_in, b_gate, gqa, gka, gqb, gkb)


def _band_mask(tq, w, n_prev):
    shift = CHUNK.bit_length() - 1
    qc = lax.shift_right_logical(lax.broadcasted_iota(jnp.int32, (tq, w), 0), shift)
    kc = lax.shift_right_logical(lax.broadcasted_iota(jnp.int32, (tq, w), 1), shift)
    d = kc - qc
    return (d >= 0) & (d <= n_prev)


def _window(reach, tq):
    return -(-(reach + tq) // LANES) * LANES


def _stack_heads(qs):
    low = _lane_is_low()
    zero = jnp.zeros_like(qs)
    return jnp.concatenate([jnp.where(low, qs, zero), jnp.where(low, zero, qs)], axis=0)


def _with_ones(v):
    return jnp.concatenate([v, jnp.ones_like(v)], axis=1)


def _normalised(o2, extra_den, rows):
    den = o2[:, LANES:]
    if extra_den is not None:
        den = den + extra_den
    o = o2[:, :LANES] / den
    return jnp.where(_lane_is_low(), o[:rows], o[rows:])


def _init_bias(slopes_ref, rel_ref, bias_a, bias_b, ts):
    n_slabs, _, wa = bias_a.shape
    wb = bias_b.shape[2]
    qpos = lax.broadcasted_iota(jnp.int32, (ts, wa), 0)
    kpos = lax.broadcasted_iota(jnp.int32, (ts, wa), 1) - A_REACH
    dist = jnp.abs(qpos - kpos).astype(_F32)
    band_a = _band_mask(ts, wa, A_PREV)
    band_b = _band_mask(ts, wb, B_PREV)
    for h in range(2 * n_slabs):
        rows_h = slice((h % 2) * ts, (h % 2 + 1) * ts)
        bias_a[h // 2, rows_h, :] = jnp.where(band_a, (-LOG2E * slopes_ref[h]) * dist, NEG_INF)
        rows = jnp.broadcast_to(rel_ref[h:h + 1, :], (ts, REL_ROW))
        rows = pltpu.roll(rows, REL_ROW - ATTN_SUB + 1, 1, stride=1, stride_axis=0)
        bias_b[h // 2, rows_h, :] = jnp.where(band_b, LOG2E * rows[:, :wb], NEG_INF)


def _attn_kernel(*refs, tq, ts, n_blocks, n_slabs):
    na, nb = n_blocks
    slopes_ref, sinks_ref, rel_ref, qa_ref, qb_ref, g_ref = refs[:6]
    kv = list(refs[6:6 + 2 * (na + nb)])
    ka_refs, va_refs, kb_refs, vb_refs = kv[:na], kv[na:2 * na], kv[2 * na:2 * na + nb], kv[2 * na + nb:]
    o_ref, bias_a, bias_b = refs[6 + len(kv):9 + len(kv)]
    bufs = refs[9 + len(kv):]
    sa_ref, sb_ref, pa_ref, pb_ref, da_ref = (bufs[2 * n:2 * n + 2] for n in range(5))
    wa, wb = bias_a.shape[2], bias_b.shape[2]
    b = pl.program_id(0)
    i = pl.program_id(1)

    @pl.when((b == 0) & (i == 0))
    def _():
        _init_bias(slopes_ref, rel_ref, bias_a, bias_b, ts)

    row_is_lo = lax.broadcasted_iota(jnp.int32, (2 * ts, 1), 0) < ts
    col_a = lax.broadcasted_iota(jnp.int32, (1, wa), 1)
    col_b = lax.broadcasted_iota(jnp.int32, (1, wb), 1)

    def window(blocks, slab, u, reach, feature_major):
        if feature_major:
            cut = lambda r, lo, hi: r[0, 0, slab, :, lo:hi]
        else:
            cut = lambda r, lo, hi: r[0, slab, lo:hi, :]
        lo, hi = u * ts - reach, (u + 1) * ts
        pieces = []
        for j, r in enumerate(blocks):
            base = (j - len(blocks) + 1) * tq
            if max(lo, base) < min(hi, base + tq):
                pieces.append(cut(r, max(lo, base) - base, min(hi, base + tq) - base))
        return pieces[0] if len(pieces) == 1 else jnp.concatenate(pieces, axis=1 if feature_major else 0)

    def softmax_weights(s, sink_col):
        m = jnp.max(s, axis=-1, keepdims=True)
        if sink_col is not None:
            m = jnp.maximum(m, sink_col)
        p = jnp.exp2(s - m).astype(_BF16)
        return p, None if sink_col is None else jnp.broadcast_to(jnp.exp2(sink_col - m), (2 * ts, LANES))

    def run(masked):
        valid = []
        for u in range(tq // ts):
            if masked:
                start = i * tq + u * ts
                valid.append((jnp.where(col_a >= A_REACH - start, 0.0, NEG_INF),
                              jnp.where(col_b >= B_REACH - start, 0.0, NEG_INF)))
            else:
                valid.append((None, None))

        def scores(k, u, par):
            rows_u = slice(u * ts, (u + 1) * ts)
            sa_ref[par][...] = jnp.dot(_stack_heads(qa_ref[0, k, rows_u]), window(ka_refs, k // 2, u, A_REACH, True),
                                       preferred_element_type=_F32)
            sb_ref[par][...] = jnp.dot(_stack_heads(qb_ref[0, k, rows_u]), window(kb_refs, k, u, B_REACH, True),
                                       preferred_element_type=_F32)

        def softmax(k, u, par):
            valid_a, valid_b = valid[u]
            sink_col = LOG2E * jnp.where(row_is_lo, sinks_ref[2 * k], sinks_ref[2 * k + 1])
            s = sa_ref[par][...] + bias_a[k]
            pa_ref[par][...], da_ref[par][...] = softmax_weights(s if valid_a is None else s + valid_a, sink_col)
            s = sb_ref[par][...] + bias_b[k]
            pb_ref[par][...], _ = softmax_weights(s if valid_b is None else s + valid_b, None)

        def values(k, u, par):
            rows_u = slice(u * ts, (u + 1) * ts)
            oa = jnp.dot(pa_ref[par][...], _with_ones(window(va_refs, k // 2, u, A_REACH, False)),
                         preferred_element_type=_F32)
            ob = jnp.dot(pb_ref[par][...], _with_ones(window(vb_refs, k, u, B_REACH, False)),
                         preferred_element_type=_F32)
            mixed = (g_ref[0, k, rows_u].astype(_F32) * _normalised(oa, da_ref[par][...], ts)
                     + g_ref[0, n_slabs + k, rows_u].astype(_F32) * _normalised(ob, None, ts))
            o_ref[0, k, rows_u] = mixed.astype(o_ref.dtype)

        n_sub = tq // ts

        def step(k, u, first_slab=False):
            for back, stage in ((2, values), (1, softmax)):
                ku, uu = (k, u - back) if u >= back else (k - 1, u - back + n_sub)
                if not (first_slab and u < back):
                    stage(ku, uu, uu % 2)
            scores(k, u, u % 2)

        for u in range(n_sub):
            step(0, u, first_slab=True)

        def body(k, carry):
            for u in range(n_sub):
                step(k, u)
            return carry

        lax.fori_loop(1, n_slabs, body, 0, unroll=4)
        values(n_slabs - 1, n_sub - 2, 0)
        softmax(n_slabs - 1, n_sub - 1, 1)
        values(n_slabs - 1, n_sub - 1, 1)

    n_early = B_REACH // tq
    pl.when(i < n_early)(lambda: run(True))
    pl.when(i >= n_early)(lambda: run(False))


def _attention(slopes, sinks, rel_rows, qa, qb, gates, ka, va, kb, vb):
    bsz, n_slabs, seq, _ = qa.shape
    tq, ts = ATTN_TILE, ATTN_SUB
    assert tq % (2 * ts) == 0 and seq % tq == 0 and B_REACH % tq == 0
    n_blocks = (1 + -(-A_REACH // tq), 1 + B_REACH // tq)

    def row(b, i):
        return (b, 0, i, 0)

    def const2(b, i):
        return (0, 0)

    def kv_specs(t, n, feature_major):
        if feature_major:
            return [pl.BlockSpec((1, 1) + t.shape[2:], lambda b, i, d=d: (b, jnp.maximum(i - d, 0), 0, 0, 0))
                    for d in range(n - 1, -1, -1)]
        return [pl.BlockSpec((1, t.shape[1], tq, LANES), lambda b, i, d=d: (b, 0, jnp.maximum(i - d, 0), 0))
                for d in range(n - 1, -1, -1)]

    smem = pl.BlockSpec(memory_space=pltpu.SMEM)
    rows = lambda ns: pl.BlockSpec((1, ns, tq, LANES), row)
    na, nb = n_blocks
    wa, wb = A_REACH + ts, B_REACH + ts
    in_specs = ([smem, smem, _resident(rel_rows.shape, const2), rows(n_slabs), rows(n_slabs), rows(2 * n_slabs)]
                + kv_specs(ka, na, True) + kv_specs(va, na, False) + kv_specs(kb, nb, True) + kv_specs(vb, nb, False))
    operands = [slopes, sinks, rel_rows, qa, qb, gates] + [ka] * na + [va] * na + [kb] * nb + [vb] * nb
    kern = functools.partial(_attn_kernel, tq=tq, ts=ts, n_blocks=n_blocks, n_slabs=n_slabs)
    return pl.pallas_call(
        kern, grid=(bsz, seq // tq), in_specs=in_specs, out_specs=rows(n_slabs),
        out_shape=jax.ShapeDtypeStruct(qa.shape, _BF16),
        scratch_shapes=[pltpu.VMEM((n_slabs, 2 * ts, wa), _F32), pltpu.VMEM((n_slabs, 2 * ts, wb), _F32)]
        + [pltpu.VMEM((2 * ts, w), dt)
           for w, dt in ((wa, _F32), (wb, _F32), (wa, _BF16), (wb, _BF16), (LANES, _F32)) for _ in range(2)],
        compiler_params=_params(2), name="attn")(*operands)


def _sample_attn_kernel(slopes_ref, sinks_ref, rel_ref, qa_ref, qb_ref, g_ref,
                        cak_ref, cav_ref, cbk_ref, cbv_ref, nak_ref, nav_ref, nbk_ref, nbv_ref,
                        o_ref, bias_a, bias_b, *, tq, n_slabs):
    @pl.when(pl.program_id(0) == 0)
    def _():
        _init_bias(slopes_ref, rel_ref, bias_a, bias_b, tq)

    low = _lane_is_low()
    row_is_lo = lax.broadcasted_iota(jnp.int32, (2 * tq, 1), 0) < tq

    def both_halves(x, half):
        sw = pltpu.roll(x, HEAD_DIM, 1)
        return (jnp.where(low, x, sw) if half == 0 else jnp.where(low, sw, x)).astype(_BF16)

    def attend(qs, parts, bias, sink_col):
        qq = _stack_heads(qs)
        ss = [lax.dot_general(qq, k, (((1,), (1,)), ((), ())), preferred_element_type=_F32)
              + bias[:, c0:c0 + k.shape[0]] for k, _, c0 in parts]
        m = functools.reduce(jnp.maximum, [jnp.max(s, axis=-1, keepdims=True) for s in ss])
        if sink_col is not None:
            m = jnp.maximum(m, sink_col)
        o2 = sum(jnp.dot(jnp.exp2(s - m).astype(_BF16), _with_ones(v), preferred_element_type=_F32)
                 for s, (_, v, _) in zip(ss, parts))
        return _normalised(o2, None if sink_col is None else jnp.exp2(sink_col - m), tq)

    parts_a = None
    for s in range(n_slabs):
        kvh = s // 2
        if s % 2 == 0:
            la = slice(kvh // 2 * LANES, (kvh // 2 + 1) * LANES)
            dup = lambda r: both_halves(r[0, :, la], kvh % 2)
            parts_a = [(dup(cak_ref), dup(cav_ref), 0), (dup(nak_ref), dup(nav_ref), A_REACH)]
        sink_col = LOG2E * jnp.where(row_is_lo, sinks_ref[2 * s], sinks_ref[2 * s + 1])
        oa = attend(qa_ref[0, s], parts_a, bias_a[s], sink_col)
        lb = slice(s * LANES, (s + 1) * LANES)
        cut = lambda r: r[0, :, lb].astype(_BF16)
        ob = attend(qb_ref[0, s], [(cut(cbk_ref), cut(cbv_ref), 0), (cut(nbk_ref), cut(nbv_ref), B_REACH)],
                    bias_b[s], None)
        mixed = g_ref[0, s].astype(_F32) * oa + g_ref[0, n_slabs + s].astype(_F32) * ob
        o_ref[0, s] = mixed.astype(o_ref.dtype)


def _sample_attention(slopes, sinks, rel_rows, qa, qb, gates, cache_a_k, cache_a_v, cache_b_k, cache_b_v,
                      new_a_k, new_a_v, new_b_k, new_b_v, *, layer, tq):
    _, n_slabs, rows, _ = qa.shape
    n_batches = rows // tq

    def q_rows(b):
        return (0, 0, b, 0)

    def cached(b):
        return (layer * n_batches + b, 0, 0)

    def new_rows(b):
        return (0, b, 0)

    smem = pl.BlockSpec(memory_space=pltpu.SMEM)
    q_blk = lambda ns: pl.BlockSpec((1, ns, tq, LANES), q_rows)
    cache_blk = lambda t: pl.BlockSpec((1,) + t.shape[1:], cached)
    new_blk = lambda t: pl.BlockSpec((1, tq, t.shape[2]), new_rows)
    caches = (cache_a_k, cache_a_v, cache_b_k, cache_b_v)
    news = (new_a_k, new_a_v, new_b_k, new_b_v)
    in_specs = ([smem, smem, _resident(rel_rows.shape, lambda b: (0, 0)), q_blk(n_slabs), q_blk(n_slabs),
                 q_blk(2 * n_slabs)] + [cache_blk(t) for t in caches] + [new_blk(t) for t in news])
    kern = functools.partial(_sample_attn_kernel, tq=tq, n_slabs=n_slabs)
    return pl.pallas_call(
        kern, grid=(n_batches,), in_specs=in_specs, out_specs=q_blk(n_slabs),
        out_shape=jax.ShapeDtypeStruct(qa.shape, _BF16),
        scratch_shapes=[pltpu.VMEM((n_slabs, 2 * tq, _window(A_REACH, tq)), _F32),
                        pltpu.VMEM((n_slabs, 2 * tq, _window(B_REACH, tq)), _F32)],
        compiler_params=_params(1), name="sample_attn")(slopes, sinks, rel_rows, qa, qb, gates, *caches, *news)


def _gelu(x):
    return 0.5 * x * (1.0 + lax.erf(x * (2.0 ** -0.5)))


def _ffn_kernel(*refs, d_ff, seg, carried):
    if carried:
        x_ref, m_ref, wo_ref, n2_ref, wu_ref, cw_ref, cb_ref, wd_ref, y_ref, st_ref, carry, act = refs
    else:
        x_ref, m_ref, wo_ref, n2_ref, wu_ref, cw_ref, cb_ref, wd_ref, prev_ref, y_ref, st_ref, act = refs
    rows = x_ref.shape[1]
    n_seg = rows // seg

    if carried:
        @pl.when(pl.program_id(1) == 0)
        def _():
            carry[...] = jnp.zeros(carry.shape, carry.dtype)

    mixed = jnp.concatenate([m_ref[0, s] for s in range(m_ref.shape[1])], axis=-1)
    x1 = x_ref[0] + jnp.dot(mixed, wo_ref[...], preferred_element_type=_F32)
    h = _rmsnorm_rows(x1, n2_ref[...]).astype(_BF16)
    r = jnp.bitwise_and(lax.broadcasted_iota(jnp.int32, (rows, 1), 0), seg - 1)

    def conv(col):
        u = jnp.dot(h, wu_ref[:, col:col + FF_CHUNK], preferred_element_type=_F32)
        cs = slice(col, col + FF_CHUNK)
        if carried:
            p0, p1 = carry[SUBLANES - 2:SUBLANES - 1, cs], carry[SUBLANES - 1:SUBLANES, cs]
            carry[:, cs] = u[rows - SUBLANES:, :]
        else:
            per_seg = lambda j: jnp.concatenate(
                [jnp.broadcast_to(prev_ref[g, j:j + 1, cs], (seg, FF_CHUNK)) for g in range(n_seg)], axis=0)
            p0, p1 = per_seg(0), per_seg(1)
        for g in range(n_seg):
            st_ref[g, :, cs] = u[(g + 1) * seg - SUBLANES:(g + 1) * seg, :]
        u1 = jnp.where(r == 0, p1, pltpu.roll(u, 1, 0))
        u2 = jnp.where(r == 0, p0, jnp.where(r == 1, p1, pltpu.roll(u, 2, 0)))
        return cb_ref[:, cs] + cw_ref[0:1, cs] * u2 + cw_ref[1:2, cs] * u1 + cw_ref[2:3, cs] * u

    for c in range(0, d_ff, FF_CHUNK):
        a = conv(c)
        g = conv(d_ff + c)
        act[:, c:c + FF_CHUNK] = (_gelu(a) * g).astype(act.dtype)

    y_ref[0] = x1 + jnp.dot(act[...], wd_ref[...], preferred_element_type=_F32)


def _ffn(x, mixed, w_out, n2, w_up, conv_w, conv_b, w_down, prev=None, *, layer, tile, seg):
    bsz, seq, d_model = x.shape
    d_ff = w_down.shape[1]
    n_tiles = seq // tile
    carried = prev is None
    n_seg = 1 if carried else tile // seg
    assert seg & (seg - 1) == 0 and (seg == seq if carried else tile % seg == 0)

    def row(b, i):
        return (b, i, 0)

    def slab_row(b, i):
        return (b, 0, i, 0)

    def segs(b, i):
        return (b if carried else b * n_tiles + i, 0, 0)

    def const2(b, i):
        return (0, 0)

    operands = [x, mixed, w_out, n2, w_up, conv_w, conv_b, w_down]
    in_specs = [pl.BlockSpec((1, tile, d_model), row),
                pl.BlockSpec((1, mixed.shape[1], tile, LANES), slab_row),
                _layer_resident(w_out, layer), _resident((1, d_model), const2),
                _layer_resident(w_up, layer), _resident(conv_w.shape, const2),
                _resident((1, 2 * d_ff), const2), _layer_resident(w_down, layer)]
    scratch = [pltpu.VMEM((tile, d_ff), _BF16)]
    if carried:
        scratch.insert(0, pltpu.VMEM((SUBLANES, 2 * d_ff), _F32))
    else:
        operands.append(prev)
        in_specs.append(pl.BlockSpec((n_seg,) + prev.shape[1:], segs))
    n_state = bsz * (1 if carried else n_tiles * n_seg)
    out_specs = (pl.BlockSpec((1, tile, d_model), row), pl.BlockSpec((n_seg, SUBLANES, 2 * d_ff), segs))
    out_shape = (jax.ShapeDtypeStruct((bsz, seq, d_model), _F32),
                 jax.ShapeDtypeStruct((n_state, SUBLANES, 2 * d_ff), _F32))
    kern = functools.partial(_ffn_kernel, d_ff=d_ff, seg=tile if carried else seg, carried=carried)
    return pl.pallas_call(
        kern, grid=(bsz, n_tiles), in_specs=in_specs, out_specs=out_specs, out_shape=out_shape,
        scratch_shapes=scratch, compiler_params=_params(2), name="ffn")(*operands)


def _rel_rows(table):
    n_low = B_REACH - REL_CLIP + ATTN_SUB - 1
    n_high = REL_ROW - n_low - table.shape[0]
    rows = jnp.concatenate([jnp.broadcast_to(table[-1:], (n_low, table.shape[1])),
                            table[::-1],
                            jnp.broadcast_to(table[:1], (n_high, table.shape[1]))], axis=0)
    return rows.T


def _tile2(g, scale=1.0):
    return (jnp.tile(g, 2) * scale).reshape(1, LANES)


def kernel(x_prompt, x_sample, cache_a_k, cache_a_v, cache_b_k, cache_b_v, cache_ffn_conv, norm1_g, w_in, b_gate, qn_a_g, kn_a_g, qn_b_g, kn_b_g, sinks_a, rel_bias_b, w_out, norm2_g, w_up, conv_w, conv_b, w_down):
    depth = w_in.shape[0]
    bsz, seq, d_model = x_prompt.shape
    dbs, dseq, _ = x_sample.shape
    n_heads = d_model // HEAD_DIM
    assert cache_a_k.shape[2] == A_REACH and cache_b_k.shape[2] == B_REACH
    assert seq % PROJ_TILE == 0 and PROJ_TILE >= B_REACH and PROJ_TILE % ATTN_TILE == 0 and dseq <= CHUNK
    slopes = 2.0 ** (-8.0 * jnp.arange(1, n_heads + 1, dtype=_F32) / n_heads)
    xp, xs = x_prompt, x_sample.reshape(1, dbs * dseq, d_model)
    heads = lambda t, rows: t.reshape(-1, rows, t.shape[-1] // HEAD_DIM, HEAD_DIM)
    w_in, w_out, w_up, w_down = (_to_bf16(w) for w in (w_in, w_out, w_up, w_down))
    flat = lambda c: c.reshape(depth * dbs, c.shape[2], -1)
    caches = [flat(c) for c in (cache_a_k, cache_a_v, cache_b_k, cache_b_v)]
    pk, sk = [], []
    for l in range(depth):
        n1, n2 = norm1_g[l].reshape(1, -1), norm2_g[l].reshape(1, -1)
        bg, cb = b_gate[l].reshape(1, -1), conv_b[l].reshape(1, -1)
        gains = (_tile2(qn_a_g[l], Q_SCALE), _tile2(kn_a_g[l]), _tile2(qn_b_g[l], Q_SCALE), _tile2(kn_b_g[l]))
        rel = _rel_rows(rel_bias_b[l])

        qa, qb, g, ka, va, kb, vb, kat, vat, kbt, vbt = _proj(
            xp, n1, w_in, bg, *gains, layer=l, tile=PROJ_TILE)
        mixed = _attention(slopes, sinks_a[l], rel, qa, qb, g, ka, va, kb, vb)
        xp, st = _ffn(xp, mixed, w_out, n2, w_up, conv_w[l], cb, w_down, layer=l, tile=PROJ_TILE, seg=seq)
        pk.append((heads(kat[:, -A_REACH:], A_REACH), heads(vat[:, -A_REACH:], A_REACH),
                   heads(kbt, B_REACH), heads(vbt, B_REACH), st[:, -(CONV_W - 1):]))

        qa, qb, g, kat, vat, kbt, vbt = _proj(
            xs, n1, w_in, bg, *gains, layer=l, tile=dbs * dseq, emit_kv=False)
        mixed = _sample_attention(slopes, sinks_a[l], rel, qa, qb, g, *caches, kat, vat, kbt, vbt, layer=l, tq=dseq)
        xs, st = _ffn(xs, mixed, w_out, n2, w_up, conv_w[l], cb, w_down, cache_ffn_conv[l],
                      layer=l, tile=dbs * dseq, seg=dseq)
        sk.append((heads(kat, dseq), heads(vat, dseq), heads(kbt, dseq), heads(vbt, dseq),
                   st[:, -(CONV_W - 1):]))

    stk = lambda states, i: jnp.stack([s[i] for s in states])
    return (xp, xs.reshape(dbs, dseq, d_model),
            stk(pk, 0), stk(pk, 1), stk(pk, 2), stk(pk, 3), stk(pk, 4),
            stk(sk, 0), stk(sk, 1), stk(sk, 2), stk(sk, 3), stk(sk, 4))
```

```python
import functools

import jax
import jax.numpy as jnp
from jax import lax
from jax.experimental import pallas as pl
from jax.experimental.pallas import tpu as pltpu

HEAD_DIM = 64
CHUNK = 64
A_GROUP = 4
A_PREV = 2
B_PREV = 8
A_REACH = A_PREV * CHUNK
B_REACH = B_PREV * CHUNK
REL_CLIP = 128
CONV_W = 3
EPS = 1e-6
NEG_INF = -1e30
LOG2E = 1.4426950408889634
Q_SCALE = HEAD_DIM ** -0.5 * LOG2E

LANES = 128
SUBLANES = 8
MXU_WIDTH = 256
VMEM_LIMIT_BYTES = 56 * 1024 * 1024

PROJ_TILE = 512
ATTN_TILE = 512
ATTN_SUB = 128
FF_CHUNK = 256
REL_ROW = 1024

_BF16 = jnp.bfloat16
_F32 = jnp.float32


def _resident(shape, index_map):
    return pl.BlockSpec(shape, index_map, pipeline_mode=pl.Buffered(1))


def _layer_resident(w, layer):
    return pl.BlockSpec((None,) + w.shape[1:], lambda *_: (layer, 0, 0), pipeline_mode=pl.Buffered(1))


def _params(n_axes, flags=None):
    return pltpu.CompilerParams(dimension_semantics=("arbitrary",) * n_axes,
                                vmem_limit_bytes=VMEM_LIMIT_BYTES, flags=flags)


def _rmsnorm_rows(x, g):
    return x * lax.rsqrt(jnp.mean(x * x, axis=-1, keepdims=True) + EPS) * g


def _lane_is_low():
    return lax.broadcasted_iota(jnp.int32, (1, LANES), 1) < HEAD_DIM


def _headnorm_slab(z, g2):
    low = _lane_is_low()
    sq = z * z
    s_lo = jnp.sum(jnp.where(low, sq, 0.0), axis=-1, keepdims=True)
    s_hi = jnp.sum(jnp.where(low, 0.0, sq), axis=-1, keepdims=True)
    ms = jnp.where(low, s_lo, s_hi) * (1.0 / HEAD_DIM)
    return z * lax.rsqrt(ms + EPS) * g2


CAST_BLOCK_BYTES = 14 * 1024 * 1024


def _cast_kernel(x_ref, o_ref):
    o_ref[...] = x_ref[...].astype(o_ref.dtype)


def _to_bf16(w):
    depth, rows, cols = w.shape
    packing = 2 * SUBLANES
    fits = [r for r in range(packing, rows + 1, packing) if rows % r == 0 and r * cols * 4 <= CAST_BLOCK_BYTES]
    rb = max(fits)
    spec = pl.BlockSpec((1, rb, cols), lambda l, i: (l, i, 0))
    return pl.pallas_call(_cast_kernel, grid=(depth, rows // rb), in_specs=[spec], out_specs=spec,
                          out_shape=jax.ShapeDtypeStruct(w.shape, _BF16), compiler_params=_params(2),
                          name="cast")(w)


def _proj_kernel(x_ref, n1_ref, w_ref, bg_ref, gqa_ref, gka_ref, gqb_ref, gkb_ref, qa_ref, qb_ref, g_ref, *outs,
                 d_model, a_kv, emit_kv):
    ka_ref, va_ref, kb_ref, vb_ref = outs[:4] if emit_kv else (None,) * 4
    kat_ref, vat_ref, kbt_ref, vbt_ref = outs[-4:]
    o_ka = d_model
    o_va = o_ka + a_kv
    o_qb = o_va + a_kv
    o_kb = o_qb + d_model
    o_vb = o_kb + d_model
    o_g = o_vb + d_model

    h = _rmsnorm_rows(x_ref[0], n1_ref[...]).astype(_BF16)
    low = _lane_is_low()

    def slabs(base, width):
        for c in range(0, width, MXU_WIDTH):
            z = jnp.dot(h, w_ref[:, base + c:base + c + MXU_WIDTH], preferred_element_type=_F32)
            for t in range(MXU_WIDTH // LANES):
                yield c // LANES + t, z[:, t * LANES:(t + 1) * LANES]

    def emit(out_ref, base, width, gain_ref=None, tail_ref=None, twice=False, feature_major=False):
        def put(idx, y):
            if feature_major:
                yt = y.T.astype(out_ref.dtype)
                for t in range(out_ref.shape[1]):
                    out_ref[0, t, idx] = yt[:, t * ATTN_TILE:(t + 1) * ATTN_TILE]
            else:
                out_ref[0, idx] = y.astype(out_ref.dtype)

        for s, y in slabs(base, width):
            if gain_ref is not None:
                y = _headnorm_slab(y, gain_ref[...])
            if tail_ref is not None:
                tail_ref[0, :, s * LANES:(s + 1) * LANES] = y
            if out_ref is None:
                continue
            if twice:
                y_sw = pltpu.roll(y, HEAD_DIM, 1)
                put(2 * s, jnp.where(low, y, y_sw))
                put(2 * s + 1, jnp.where(low, y_sw, y))
            else:
                put(s, y)

    emit(qa_ref, 0, d_model, gqa_ref)
    emit(ka_ref, o_ka, a_kv, gka_ref, kat_ref, twice=True, feature_major=True)
    emit(qb_ref, o_qb, d_model, gqb_ref)
    emit(kb_ref, o_kb, d_model, gkb_ref, kbt_ref, feature_major=True)
    for s, y in slabs(o_g, 2 * d_model):
        c = s * LANES
        g_ref[0, s] = jax.nn.sigmoid(y + bg_ref[:, c:c + LANES]).astype(g_ref.dtype)
    emit(va_ref, o_va, a_kv, None, vat_ref, twice=True)
    emit(vb_ref, o_vb, d_model, None, vbt_ref)


def _proj(x, n1, w_in, b_gate, gqa, gka, gqb, gkb, *, layer, tile, emit_kv=True):
    bsz, seq, d_model = x.shape
    a_kv = d_model // A_GROUP
    ns_d, ns_kv = d_model // LANES, a_kv // LANES

    def row(b, j):
        return (b, j, 0)

    def slab_row(b, j):
        return (b, 0, j, 0)

    def key_blocks(b, j):
        return (b, j, 0, 0, 0)

    def const2(b, j):
        return (0, 0)

    def tail(b, j):
        return (b, 0, 0)

    bf = lambda ns: jax.ShapeDtypeStruct((bsz, ns, seq, LANES), _BF16)
    bf_t = lambda ns: jax.ShapeDtypeStruct((bsz, seq // ATTN_TILE, ns, LANES, ATTN_TILE), _BF16)
    f32 = lambda ns: jax.ShapeDtypeStruct((bsz, tile, ns * LANES), _F32)
    blk = lambda ns: pl.BlockSpec((1, ns, tile, LANES), slab_row)
    blk_t = lambda ns: pl.BlockSpec((1, tile // ATTN_TILE, ns, LANES, ATTN_TILE), key_blocks)
    blk_tail = lambda ns: pl.BlockSpec((1, tile, ns * LANES), tail)
    out_shape = [bf(ns_d), bf(ns_d), bf(2 * ns_d)]
    out_specs = [blk(ns_d), blk(ns_d), blk(2 * ns_d)]
    if emit_kv:
        out_shape += [bf_t(2 * ns_kv), bf(2 * ns_kv), bf_t(ns_d), bf(ns_d)]
        out_specs += [blk_t(2 * ns_kv), blk(2 * ns_kv), blk_t(ns_d), blk(ns_d)]
    out_shape += [f32(ns_kv), f32(ns_kv), f32(ns_d), f32(ns_d)]
    out_specs += [blk_tail(ns_kv), blk_tail(ns_kv), blk_tail(ns_d), blk_tail(ns_d)]
    in_specs = [pl.BlockSpec((1, tile, d_model), row),
                _resident((1, d_model), const2),
                _layer_resident(w_in, layer),
                _resident((1, 2 * d_model), const2),
                _resident((1, LANES), const2), _resident((1, LANES), const2),
                _resident((1, LANES), const2), _resident((1, LANES), const2)]
    kern = functools.partial(_proj_kernel, d_model=d_model, a_kv=a_kv, emit_kv=emit_kv)
    return pl.pallas_call(kern, grid=(bsz, seq // tile), in_specs=in_specs, out_specs=tuple(out_specs),
                          out_shape=tuple(out_shape), compiler_params=_params(2), name="proj")(
                              x, n1, w_in, b_gate, gqa, gka, gqb, gkb)


def _band_mask(tq, w, n_prev):
    shift = CHUNK.bit_length() - 1
    qc = lax.shift_right_logical(lax.broadcasted_iota(jnp.int32, (tq, w), 0), shift)
    kc = lax.shift_right_logical(lax.broadcasted_iota(jnp.int32, (tq, w), 1), shift)
    d = kc - qc
    return (d >= 0) & (d <= n_prev)


def _window(reach, tq):
    return -(-(reach + tq) // LANES) * LANES


def _stack_heads(qs):
    low = _lane_is_low()
    zero = jnp.zeros_like(qs)
    return jnp.concatenate([jnp.where(low, qs, zero), jnp.where(low, zero, qs)], axis=0)


def _with_ones(v):
    return jnp.concatenate([v, jnp.ones_like(v)], axis=1)


def _normalised(o2, extra_den, rows):
    den = o2[:, LANES:]
    if extra_den is not None:
        den = den + extra_den
    o = o2[:, :LANES] / den
    return jnp.where(_lane_is_low(), o[:rows], o[rows:])


def _init_bias(slopes_ref, rel_ref, bias_a, bias_b, ts):
    n_slabs, _, wa = bias_a.shape
    wb = bias_b.shape[2]
    qpos = lax.broadcasted_iota(jnp.int32, (ts, wa), 0)
    kpos = lax.broadcasted_iota(jnp.int32, (ts, wa), 1) - A_REACH
    dist = jnp.abs(qpos - kpos).astype(_F32)
    band_a = _band_mask(ts, wa, A_PREV)
    band_b = _band_mask(ts, wb, B_PREV)
    for h in range(2 * n_slabs):
        rows_h = slice((h % 2) * ts, (h % 2 + 1) * ts)
        bias_a[h // 2, rows_h, :] = jnp.where(band_a, (-LOG2E * slopes_ref[h]) * dist, NEG_INF)
        rows = jnp.broadcast_to(rel_ref[h:h + 1, :], (ts, REL_ROW))
        rows = pltpu.roll(rows, REL_ROW - ATTN_SUB + 1, 1, stride=1, stride_axis=0)
        bias_b[h // 2, rows_h, :] = jnp.where(band_b, LOG2E * rows[:, :wb], NEG_INF)


def _attn_kernel(*refs, tq, ts, n_blocks, n_slabs):
    na, nb = n_blocks
    slopes_ref, sinks_ref, rel_ref, qa_ref, qb_ref, g_ref = refs[:6]
    kv = list(refs[6:6 + 2 * (na + nb)])
    ka_refs, va_refs, kb_refs, vb_refs = kv[:na], kv[na:2 * na], kv[2 * na:2 * na + nb], kv[2 * na + nb:]
    o_ref, bias_a, bias_b = refs[6 + len(kv):9 + len(kv)]
    bufs = refs[9 + len(kv):]
    sa_ref, sb_ref, pa_ref, pb_ref, da_ref = (bufs[2 * n:2 * n + 2] for n in range(5))
    wa, wb = bias_a.shape[2], bias_b.shape[2]
    b = pl.program_id(0)
    i = pl.program_id(1)

    @pl.when((b == 0) & (i == 0))
    def _():
        _init_bias(slopes_ref, rel_ref, bias_a, bias_b, ts)

    row_is_lo = lax.broadcasted_iota(jnp.int32, (2 * ts, 1), 0) < ts
    col_a = lax.broadcasted_iota(jnp.int32, (1, wa), 1)
    col_b = lax.broadcasted_iota(jnp.int32, (1, wb), 1)

    def window(blocks, slab, u, reach, feature_major):
        if feature_major:
            cut = lambda r, lo, hi: r[0, 0, slab, :, lo:hi]
        else:
            cut = lambda r, lo, hi: r[0, slab, lo:hi, :]
        lo, hi = u * ts - reach, (u + 1) * ts
        pieces = []
        for j, r in enumerate(blocks):
            base = (j - len(blocks) + 1) * tq
            if max(lo, base) < min(hi, base + tq):
                pieces.append(cut(r, max(lo, base) - base, min(hi, base + tq) - base))
        return pieces[0] if len(pieces) == 1 else jnp.concatenate(pieces, axis=1 if feature_major else 0)

    def softmax_weights(s, sink_col):
        m = jnp.max(s, axis=-1, keepdims=True)
        if sink_col is not None:
            m = jnp.maximum(m, sink_col)
        p = jnp.exp2(s - m).astype(_BF16)
        return p, None if sink_col is None else jnp.broadcast_to(jnp.exp2(sink_col - m), (2 * ts, LANES))

    def run(masked):
        valid = []
        for u in range(tq // ts):
            if masked:
                start = i * tq + u * ts
                valid.append((jnp.where(col_a >= A_REACH - start, 0.0, NEG_INF),
                              jnp.where(col_b >= B_REACH - start, 0.0, NEG_INF)))
            else:
                valid.append((None, None))

        def scores(k, u, par):
            rows_u = slice(u * ts, (u + 1) * ts)
            sa_ref[par][...] = jnp.dot(_stack_heads(qa_ref[0, k, rows_u]), window(ka_refs, k // 2, u, A_REACH, True),
                                       preferred_element_type=_F32)
            sb_ref[par][...] = jnp.dot(_stack_heads(qb_ref[0, k, rows_u]), window(kb_refs, k, u, B_REACH, True),
                                       preferred_element_type=_F32)

        def softmax(k, u, par):
            valid_a, valid_b = valid[u]
            sink_col = LOG2E * jnp.where(row_is_lo, sinks_ref[2 * k], sinks_ref[2 * k + 1])
            s = sa_ref[par][...] + bias_a[k]
            pa_ref[par][...], da_ref[par][...] = softmax_weights(s if valid_a is None else s + valid_a, sink_col)
            s = sb_ref[par][...] + bias_b[k]
            pb_ref[par][...], _ = softmax_weights(s if valid_b is None else s + valid_b, None)

        def values(k, u, par):
            rows_u = slice(u * ts, (u + 1) * ts)
            oa = jnp.dot(pa_ref[par][...], _with_ones(window(va_refs, k // 2, u, A_REACH, False)),
                         preferred_element_type=_F32)
            ob = jnp.dot(pb_ref[par][...], _with_ones(window(vb_refs, k, u, B_REACH, False)),
                         preferred_element_type=_F32)
            mixed = (g_ref[0, k, rows_u].astype(_F32) * _normalised(oa, da_ref[par][...], ts)
                     + g_ref[0, n_slabs + k, rows_u].astype(_F32) * _normalised(ob, None, ts))
            o_ref[0, k, rows_u] = mixed.astype(o_ref.dtype)

        n_sub = tq // ts

        def step(k, u, first_slab=False):
            for back, stage in ((2, values), (1, softmax)):
                ku, uu = (k, u - back) if u >= back else (k - 1, u - back + n_sub)
                if not (first_slab and u < back):
                    stage(ku, uu, uu % 2)
            scores(k, u, u % 2)

        for u in range(n_sub):
            step(0, u, first_slab=True)

        def body(k, carry):
            for u in range(n_sub):
                step(k, u)
            return carry

        lax.fori_loop(1, n_slabs, body, 0, unroll=not masked)
        values(n_slabs - 1, n_sub - 2, 0)
        softmax(n_slabs - 1, n_sub - 1, 1)
        values(n_slabs - 1, n_sub - 1, 1)

    n_early = B_REACH // tq
    pl.when(i < n_early)(lambda: run(True))
    pl.when(i >= n_early)(lambda: run(False))


def _attention(slopes, sinks, rel_rows, qa, qb, gates, ka, va, kb, vb):
    bsz, n_slabs, seq, _ = qa.shape
    tq, ts = ATTN_TILE, ATTN_SUB
    assert tq % (2 * ts) == 0 and seq % tq == 0 and B_REACH % tq == 0
    n_blocks = (1 + -(-A_REACH // tq), 1 + B_REACH // tq)

    def row(b, i):
        return (b, 0, i, 0)

    def const2(b, i):
        return (0, 0)

    def kv_specs(t, n, feature_major):
        if feature_major:
            return [pl.BlockSpec((1, 1) + t.shape[2:], lambda b, i, d=d: (b, jnp.maximum(i - d, 0), 0, 0, 0))
                    for d in range(n - 1, -1, -1)]
        return [pl.BlockSpec((1, t.shape[1], tq, LANES), lambda b, i, d=d: (b, 0, jnp.maximum(i - d, 0), 0))
                for d in range(n - 1, -1, -1)]

    smem = pl.BlockSpec(memory_space=pltpu.SMEM)
    rows = lambda ns: pl.BlockSpec((1, ns, tq, LANES), row)
    na, nb = n_blocks
    wa, wb = A_REACH + ts, B_REACH + ts
    in_specs = ([smem, smem, _resident(rel_rows.shape, const2), rows(n_slabs), rows(n_slabs), rows(2 * n_slabs)]
                + kv_specs(ka, na, True) + kv_specs(va, na, False) + kv_specs(kb, nb, True) + kv_specs(vb, nb, False))
    operands = [slopes, sinks, rel_rows, qa, qb, gates] + [ka] * na + [va] * na + [kb] * nb + [vb] * nb
    kern = functools.partial(_attn_kernel, tq=tq, ts=ts, n_blocks=n_blocks, n_slabs=n_slabs)
    return pl.pallas_call(
        kern, grid=(bsz, seq // tq), in_specs=in_specs, out_specs=rows(n_slabs),
        out_shape=jax.ShapeDtypeStruct(qa.shape, _BF16),
        scratch_shapes=[pltpu.VMEM((n_slabs, 2 * ts, wa), _F32), pltpu.VMEM((n_slabs, 2 * ts, wb), _F32)]
        + [pltpu.VMEM((2 * ts, w), dt)
           for w, dt in ((wa, _F32), (wb, _F32), (wa, _BF16), (wb, _BF16), (LANES, _F32)) for _ in range(2)],
        compiler_params=_params(2), name="attn")(*operands)


def _sample_attn_kernel(slopes_ref, sinks_ref, rel_ref, qa_ref, qb_ref, g_ref,
                        cak_ref, cav_ref, cbk_ref, cbv_ref, nak_ref, nav_ref, nbk_ref, nbv_ref,
                        o_ref, bias_a, bias_b, *, tq, n_slabs):
    @pl.when(pl.program_id(0) == 0)
    def _():
        _init_bias(slopes_ref, rel_ref, bias_a, bias_b, tq)

    low = _lane_is_low()
    row_is_lo = lax.broadcasted_iota(jnp.int32, (2 * tq, 1), 0) < tq

    def both_halves(x, half):
        sw = pltpu.roll(x, HEAD_DIM, 1)
        return (jnp.where(low, x, sw) if half == 0 else jnp.where(low, sw, x)).astype(_BF16)

    def attend(qs, parts, bias, sink_col):
        qq = _stack_heads(qs)
        ss = [lax.dot_general(qq, k, (((1,), (1,)), ((), ())), preferred_element_type=_F32)
              + bias[:, c0:c0 + k.shape[0]] for k, _, c0 in parts]
        m = functools.reduce(jnp.maximum, [jnp.max(s, axis=-1, keepdims=True) for s in ss])
        if sink_col is not None:
            m = jnp.maximum(m, sink_col)
        o2 = sum(jnp.dot(jnp.exp2(s - m).astype(_BF16), _with_ones(v), preferred_element_type=_F32)
                 for s, (_, v, _) in zip(ss, parts))
        return _normalised(o2, None if sink_col is None else jnp.exp2(sink_col - m), tq)

    parts_a = None
    for s in range(n_slabs):
        kvh = s // 2
        if s % 2 == 0:
            la = slice(kvh // 2 * LANES, (kvh // 2 + 1) * LANES)
            dup = lambda r: both_halves(r[0, :, la], kvh % 2)
            parts_a = [(dup(cak_ref), dup(cav_ref), 0), (dup(nak_ref), dup(nav_ref), A_REACH)]
        sink_col = LOG2E * jnp.where(row_is_lo, sinks_ref[2 * s], sinks_ref[2 * s + 1])
        oa = attend(qa_ref[0, s], parts_a, bias_a[s], sink_col)
        lb = slice(s * LANES, (s + 1) * LANES)
        cut = lambda r: r[0, :, lb].astype(_BF16)
        ob = attend(qb_ref[0, s], [(cut(cbk_ref), cut(cbv_ref), 0), (cut(nbk_ref), cut(nbv_ref), B_REACH)],
                    bias_b[s], None)
        mixed = g_ref[0, s].astype(_F32) * oa + g_ref[0, n_slabs + s].astype(_F32) * ob
        o_ref[0, s] = mixed.astype(o_ref.dtype)


def _sample_attention(slopes, sinks, rel_rows, qa, qb, gates, cache_a_k, cache_a_v, cache_b_k, cache_b_v,
                      new_a_k, new_a_v, new_b_k, new_b_v, *, layer, tq):
    _, n_slabs, rows, _ = qa.shape
    n_batches = rows // tq

    def q_rows(b):
        return (0, 0, b, 0)

    def cached(b):
        return (layer * n_batches + b, 0, 0)

    def new_rows(b):
        return (0, b, 0)

    smem = pl.BlockSpec(memory_space=pltpu.SMEM)
    q_blk = lambda ns: pl.BlockSpec((1, ns, tq, LANES), q_rows)
    cache_blk = lambda t: pl.BlockSpec((1,) + t.shape[1:], cached)
    new_blk = lambda t: pl.BlockSpec((1, tq, t.shape[2]), new_rows)
    caches = (cache_a_k, cache_a_v, cache_b_k, cache_b_v)
    news = (new_a_k, new_a_v, new_b_k, new_b_v)
    in_specs = ([smem, smem, _resident(rel_rows.shape, lambda b: (0, 0)), q_blk(n_slabs), q_blk(n_slabs),
                 q_blk(2 * n_slabs)] + [cache_blk(t) for t in caches] + [new_blk(t) for t in news])
    kern = functools.partial(_sample_attn_kernel, tq=tq, n_slabs=n_slabs)
    return pl.pallas_call(
        kern, grid=(n_batches,), in_specs=in_specs, out_specs=q_blk(n_slabs),
        out_shape=jax.ShapeDtypeStruct(qa.shape, _BF16),
        scratch_shapes=[pltpu.VMEM((n_slabs, 2 * tq, _window(A_REACH, tq)), _F32),
                        pltpu.VMEM((n_slabs, 2 * tq, _window(B_REACH, tq)), _F32)],
        compiler_params=_params(1), name="sample_attn")(slopes, sinks, rel_rows, qa, qb, gates, *caches, *news)


def _gelu(x):
    return 0.5 * x * (1.0 + lax.erf(x * (2.0 ** -0.5)))


def _ffn_kernel(*refs, d_ff, seg, carried):
    if carried:
        x_ref, m_ref, wo_ref, n2_ref, wu_ref, cw_ref, cb_ref, wd_ref, y_ref, st_ref, carry, act = refs
    else:
        x_ref, m_ref, wo_ref, n2_ref, wu_ref, cw_ref, cb_ref, wd_ref, prev_ref, y_ref, st_ref, act = refs
    rows = x_ref.shape[1]
    n_seg = rows // seg

    if carried:
        @pl.when(pl.program_id(1) == 0)
        def _():
            carry[...] = jnp.zeros(carry.shape, carry.dtype)

    mixed = jnp.concatenate([m_ref[0, s] for s in range(m_ref.shape[1])], axis=-1)
    x1 = x_ref[0] + jnp.dot(mixed, wo_ref[...], preferred_element_type=_F32)
    h = _rmsnorm_rows(x1, n2_ref[...]).astype(_BF16)
    r = jnp.bitwise_and(lax.broadcasted_iota(jnp.int32, (rows, 1), 0), seg - 1)

    def conv(col):
        u = jnp.dot(h, wu_ref[:, col:col + FF_CHUNK], preferred_element_type=_F32)
        cs = slice(col, col + FF_CHUNK)
        if carried:
            p0, p1 = carry[SUBLANES - 2:SUBLANES - 1, cs], carry[SUBLANES - 1:SUBLANES, cs]
            carry[:, cs] = u[rows - SUBLANES:, :]
        else:
            per_seg = lambda j: jnp.concatenate(
                [jnp.broadcast_to(prev_ref[g, j:j + 1, cs], (seg, FF_CHUNK)) for g in range(n_seg)], axis=0)
            p0, p1 = per_seg(0), per_seg(1)
        for g in range(n_seg):
            st_ref[g, :, cs] = u[(g + 1) * seg - SUBLANES:(g + 1) * seg, :]
        u1 = jnp.where(r == 0, p1, pltpu.roll(u, 1, 0))
        u2 = jnp.where(r == 0, p0, jnp.where(r == 1, p1, pltpu.roll(u, 2, 0)))
        return cb_ref[:, cs] + cw_ref[0:1, cs] * u2 + cw_ref[1:2, cs] * u1 + cw_ref[2:3, cs] * u

    for c in range(0, d_ff, FF_CHUNK):
        a = conv(c)
        g = conv(d_ff + c)
        act[:, c:c + FF_CHUNK] = (_gelu(a) * g).astype(act.dtype)

    y_ref[0] = x1 + jnp.dot(act[...], wd_ref[...], preferred_element_type=_F32)


def _ffn(x, mixed, w_out, n2, w_up, conv_w, conv_b, w_down, prev=None, *, layer, tile, seg):
    bsz, seq, d_model = x.shape
    d_ff = w_down.shape[1]
    n_tiles = seq // tile
    carried = prev is None
    n_seg = 1 if carried else tile // seg
    assert seg & (seg - 1) == 0 and (seg == seq if carried else tile % seg == 0)

    def row(b, i):
        return (b, i, 0)

    def slab_row(b, i):
        return (b, 0, i, 0)

    def segs(b, i):
        return (b if carried else b * n_tiles + i, 0, 0)

    def const2(b, i):
        return (0, 0)

    operands = [x, mixed, w_out, n2, w_up, conv_w, conv_b, w_down]
    in_specs = [pl.BlockSpec((1, tile, d_model), row),
                pl.BlockSpec((1, mixed.shape[1], tile, LANES), slab_row),
                _layer_resident(w_out, layer), _resident((1, d_model), const2),
                _layer_resident(w_up, layer), _resident(conv_w.shape, const2),
                _resident((1, 2 * d_ff), const2), _layer_resident(w_down, layer)]
    scratch = [pltpu.VMEM((tile, d_ff), _BF16)]
    if carried:
        scratch.insert(0, pltpu.VMEM((SUBLANES, 2 * d_ff), _F32))
    else:
        operands.append(prev)
        in_specs.append(pl.BlockSpec((n_seg,) + prev.shape[1:], segs))
    n_state = bsz * (1 if carried else n_tiles * n_seg)
    out_specs = (pl.BlockSpec((1, tile, d_model), row), pl.BlockSpec((n_seg, SUBLANES, 2 * d_ff), segs))
    out_shape = (jax.ShapeDtypeStruct((bsz, seq, d_model), _F32),
                 jax.ShapeDtypeStruct((n_state, SUBLANES, 2 * d_ff), _F32))
    kern = functools.partial(_ffn_kernel, d_ff=d_ff, seg=tile if carried else seg, carried=carried)
    return pl.pallas_call(
        kern, grid=(bsz, n_tiles), in_specs=in_specs, out_specs=out_specs, out_shape=out_shape,
        scratch_shapes=scratch, compiler_params=_params(2), name="ffn")(*operands)


def _rel_rows(table):
    n_low = B_REACH - REL_CLIP + ATTN_SUB - 1
    n_high = REL_ROW - n_low - table.shape[0]
    rows = jnp.concatenate([jnp.broadcast_to(table[-1:], (n_low, table.shape[1])),
                            table[::-1],
                            jnp.broadcast_to(table[:1], (n_high, table.shape[1]))], axis=0)
    return rows.T


def _tile2(g, scale=1.0):
    return (jnp.tile(g, 2) * scale).reshape(1, LANES)


def kernel(x_prompt, x_sample, cache_a_k, cache_a_v, cache_b_k, cache_b_v, cache_ffn_conv, norm1_g, w_in, b_gate, qn_a_g, kn_a_g, qn_b_g, kn_b_g, sinks_a, rel_bias_b, w_out, norm2_g, w_up, conv_w, conv_b, w_down):
    depth = w_in.shape[0]
    bsz, seq, d_model = x_prompt.shape
    dbs, dseq, _ = x_sample.shape
    n_heads = d_model // HEAD_DIM
    assert cache_a_k.shape[2] == A_REACH and cache_b_k.shape[2] == B_REACH
    assert seq % PROJ_TILE == 0 and PROJ_TILE >= B_REACH and PROJ_TILE % ATTN_TILE == 0 and dseq <= CHUNK
    slopes = 2.0 ** (-8.0 * jnp.arange(1, n_heads + 1, dtype=_F32) / n_heads)
    xp, xs = x_prompt, x_sample.reshape(1, dbs * dseq, d_model)
    heads = lambda t, rows: t.reshape(-1, rows, t.shape[-1] // HEAD_DIM, HEAD_DIM)
    w_in, w_out, w_up, w_down = (_to_bf16(w) for w in (w_in, w_out, w_up, w_down))
    flat = lambda c: c.reshape(depth * dbs, c.shape[2], -1)
    caches = [flat(c) for c in (cache_a_k, cache_a_v, cache_b_k, cache_b_v)]
    pk, sk = [], []
    for l in range(depth):
        n1, n2 = norm1_g[l].reshape(1, -1), norm2_g[l].reshape(1, -1)
        bg, cb = b_gate[l].reshape(1, -1), conv_b[l].reshape(1, -1)
        gains = (_tile2(qn_a_g[l], Q_SCALE), _tile2(kn_a_g[l]), _tile2(qn_b_g[l], Q_SCALE), _tile2(kn_b_g[l]))
        rel = _rel_rows(rel_bias_b[l])

        qa, qb, g, ka, va, kb, vb, kat, vat, kbt, vbt = _proj(
            xp, n1, w_in, bg, *gains, layer=l, tile=PROJ_TILE)
        mixed = _attention(slopes, sinks_a[l], rel, qa, qb, g, ka, va, kb, vb)
        xp, st = _ffn(xp, mixed, w_out, n2, w_up, conv_w[l], cb, w_down, layer=l, tile=PROJ_TILE, seg=seq)
        pk.append((heads(kat[:, -A_REACH:], A_REACH), heads(vat[:, -A_REACH:], A_REACH),
                   heads(kbt, B_REACH), heads(vbt, B_REACH), st[:, -(CONV_W - 1):]))

        qa, qb, g, kat, vat, kbt, vbt = _proj(
            xs, n1, w_in, bg, *gains, layer=l, tile=dbs * dseq, emit_kv=False)
        mixed = _sample_attention(slopes, sinks_a[l], rel, qa, qb, g, *caches, kat, vat, kbt, vbt, layer=l, tq=dseq)
        xs, st = _ffn(xs, mixed, w_out, n2, w_up, conv_w[l], cb, w_down, cache_ffn_conv[l],
                      layer=l, tile=dbs * dseq, seg=dseq)
        sk.append((heads(kat, dseq), heads(vat, dseq), heads(kbt, dseq), heads(vbt, dseq),
                   st[:, -(CONV_W - 1):]))

    stk = lambda states, i: jnp.stack([s[i] for s in states])
    return (xp, xs.reshape(dbs, dseq, d_model),
            stk(pk, 0), stk(pk, 1), stk(pk, 2), stk(pk, 3), stk(pk, 4),
            stk(sk, 0), stk(sk, 1), stk(sk, 2), stk(sk, 3), stk(sk, 4))
```
